```python
import jax, jax.numpy as jnp
from jax import lax
import numpy as np

D_MODEL = 4096
BATCH = 2
SEQ = 4096
DEPTH = 2

HEAD_DIM = 128
N_HEADS = D_MODEL // HEAD_DIM
H_DIL = 3 * N_HEADS // 8
H_NSA = N_HEADS // 4
H_FOX = N_HEADS - H_DIL - H_NSA
NSA_KV_GROUPS = 2
NSA_REP = H_NSA // NSA_KV_GROUPS
DILATED_PATTERNS = ((128, 1), (512, 4), (2048, 16))
CMP_BLOCK = 32
CMP_STRIDE = 16
CMP_HIDDEN = HEAD_DIM
SEL_BLOCK = 64
N_SELECT = 16
NSA_WINDOW = 512
N_NSA_BRANCHES = 3
Q_BLOCK = 128
FORCE_SCORE = 1e9
NEG_INF = -1e30
D_FF = -(-8 * D_MODEL // (3 * 256)) * 256

IN_SPLITS = (
    H_DIL * HEAD_DIM, H_DIL * HEAD_DIM, H_DIL * HEAD_DIM,
    H_NSA * HEAD_DIM,
    NSA_KV_GROUPS * HEAD_DIM, NSA_KV_GROUPS * HEAD_DIM,
    NSA_KV_GROUPS * HEAD_DIM, NSA_KV_GROUPS * HEAD_DIM,
    NSA_KV_GROUPS * HEAD_DIM, NSA_KV_GROUPS * HEAD_DIM,
    H_NSA * N_NSA_BRANCHES,
    H_FOX * HEAD_DIM, H_FOX * HEAD_DIM, H_FOX * HEAD_DIM,
    H_FOX,
)
D_IN = sum(IN_SPLITS)

kernel_name = 'hymba_style_dilated_nsa_fox_block'


def rms_norm(x, g, eps=1e-6):
    xf = x.astype(jnp.float32)
    y = xf * lax.rsqrt(jnp.mean(xf * xf, axis=-1, keepdims=True) + eps)
    return (y * g.astype(jnp.float32)).astype(x.dtype)


def alibi_slopes():
    n = H_DIL + H_NSA
    s = 2.0 ** (-8.0 * np.arange(1, n + 1) / n)
    nsa_mask = np.isin(np.arange(n) % 5, [1, 3])
    return (jnp.asarray(s[~nsa_mask], jnp.float32),
            jnp.asarray(s[nsa_mask], jnp.float32).reshape(NSA_KV_GROUPS, NSA_REP))


def banded_attention(q, k, v, window, slope, dist_scale):
    n, hd = q.shape[-2:]
    nb = -(-n // window)
    pad = nb * window - n

    def blocks(a):
        a = jnp.pad(a, [(0, 0)] * (a.ndim - 2) + [(0, pad), (0, 0)])
        return a.reshape(a.shape[:-2] + (nb, window, hd))

    def with_prev(a):
        prev = jnp.pad(a, [(0, 0)] * (a.ndim - 3) + [(1, 0), (0, 0), (0, 0)])[..., :-1, :, :]
        return jnp.concatenate([prev, a], axis=-2)

    qb = blocks(q)
    kb = with_prev(blocks(k))
    vb = with_prev(blocks(v))
    s = jnp.matmul(qb, jnp.swapaxes(kb, -1, -2)).astype(jnp.float32) * hd ** -0.5
    qi = jnp.arange(window)[:, None]
    kj = jnp.arange(2 * window)[None, :]
    dist = qi + window - kj
    key_pos = (jnp.arange(nb) * window)[:, None, None] - window + kj
    valid = (dist >= 0) & (dist <= window) & (key_pos >= 0)
    s = s - slope[..., None, None, None] * (dist * dist_scale).astype(jnp.float32)
    s = jnp.where(valid, s, NEG_INF)
    m = jnp.max(s, axis=-1, keepdims=True)
    p = jnp.exp(s - m)
    den = jnp.sum(p, axis=-1, keepdims=True)
    o = jnp.matmul((p / den).astype(v.dtype), vb)
    lse = (m + jnp.log(den))[..., 0]
    o = o.reshape(o.shape[:-3] + (nb * window, hd))[..., :n, :]
    lse = lse.reshape(lse.shape[:-2] + (nb * window,))[..., :n]
    return o, lse


def dilated_attention(q, k, v, slopes):
    B, H, L, hd = q.shape
    outs, lses = [], []
    for window, dil in DILATED_PATTERNS:
        def by_stride(a):
            return a.reshape(B, H, L // dil, dil, hd).transpose(0, 1, 3, 2, 4)
        o, lse = banded_attention(by_stride(q), by_stride(k), by_stride(v),
                                  window // dil, slopes[None, :, None], dil)
        outs.append(o.transpose(0, 1, 3, 2, 4).reshape(B, H, L, hd))
        lses.append(lse.transpose(0, 1, 3, 2).reshape(B, H, L))
    w = jax.nn.softmax(jnp.stack(lses), axis=0)
    return jnp.einsum('pbhl,pbhld->bhld', w.astype(q.dtype), jnp.stack(outs))


def compress_blocks(a, pe, w1, w2):
    B, G, L, hd = a.shape
    n_chunk = L // CMP_STRIDE
    n_sub = CMP_BLOCK // CMP_STRIDE
    n_cmp = n_chunk - n_sub + 1
    chunks = a.reshape(B, G, n_chunk, CMP_STRIDE, hd)
    blocks = jnp.concatenate([chunks[:, :, i:i + n_cmp] for i in range(n_sub)], axis=3)
    blocks = (blocks + pe).reshape(B, G, n_cmp, CMP_BLOCK * hd)
    return jax.nn.gelu(blocks @ w1) @ w2


def native_sparse_attention(q, k_cmp, v_cmp, k_slc, v_slc, k_win, v_win, gates, slopes):
    B, G, R, L, hd = q.shape
    scale = hd ** -0.5
    t = jnp.arange(L)
    sl = slopes[None, :, :, None, None]

    n_cmp = k_cmp.shape[2]
    c_end = jnp.arange(n_cmp) * CMP_STRIDE + CMP_BLOCK - 1
    c_dist = (t[:, None] - c_end[None, :]).astype(jnp.float32)
    c_valid = c_dist >= 0
    s = jnp.einsum('bgrtd,bgcd->bgrtc', q, k_cmp).astype(jnp.float32) * scale - sl * c_dist
    p_cmp = jax.nn.softmax(jnp.where(c_valid, s, NEG_INF), axis=-1) * jnp.any(c_valid, axis=-1, keepdims=True)
    o_cmp = jnp.einsum('bgrtc,bgcd->bgrtd', p_cmp.astype(v_cmp.dtype), v_cmp)

    n_sel = L // SEL_BLOCK
    c_start = np.arange(n_cmp) * CMP_STRIDE
    s_start = np.arange(n_sel) * SEL_BLOCK
    overlap = ((c_start[:, None] < s_start[None, :] + SEL_BLOCK)
               & (c_start[:, None] + CMP_BLOCK > s_start[None, :])).astype(np.float32)
    imp = jnp.einsum('bgtc,cj->bgtj', p_cmp.sum(axis=2), jnp.asarray(overlap))
    blk = jnp.arange(n_sel)[None, :]
    cur = t[:, None] // SEL_BLOCK
    forced = (blk == 0) | (blk == cur) | (blk == cur - 1)
    imp = jnp.where(forced, FORCE_SCORE, jnp.where(blk <= cur, imp, -1.0))
    k_sel = min(N_SELECT, n_sel)
    _, sel_idx = lax.top_k(imp, k_sel)

    nqb = L // Q_BLOCK
    q_blk = q.reshape(B, G, R, nqb, Q_BLOCK, hd).transpose(3, 0, 1, 2, 4, 5)
    idx_blk = sel_idx.reshape(B, G, nqb, Q_BLOCK, k_sel).transpose(2, 0, 1, 3, 4)
    t_blk = t.reshape(nqb, Q_BLOCK)
    gather = jax.vmap(jax.vmap(lambda a, i: a[i]))
    n_keys = k_sel * SEL_BLOCK

    def sel_block(args):
        qb, ib, tb = args
        tok = (ib[..., None] * SEL_BLOCK + jnp.arange(SEL_BLOCK)).reshape(B, G, Q_BLOCK * n_keys)
        kg = gather(k_slc, tok).reshape(B, G, Q_BLOCK, n_keys, hd)
        vg = gather(v_slc, tok).reshape(B, G, Q_BLOCK, n_keys, hd)
        dist = (tb[:, None] - tok.reshape(B, G, Q_BLOCK, n_keys)).astype(jnp.float32)[:, :, None]
        s = jnp.einsum('bgrqd,bgqkd->bgrqk', qb, kg).astype(jnp.float32) * scale - sl * dist
        p = jax.nn.softmax(jnp.where(dist >= 0, s, NEG_INF), axis=-1)
        return jnp.einsum('bgrqk,bgqkd->bgrqd', p.astype(vg.dtype), vg)

    o_slc = lax.map(sel_block, (q_blk, idx_blk, t_blk))
    o_slc = o_slc.transpose(1, 2, 3, 0, 4, 5).reshape(B, G, R, L, hd)

    o_win, _ = banded_attention(q, k_win[:, :, None], v_win[:, :, None], NSA_WINDOW, slopes[None], 1)

    g = gates.reshape(B, L, G, R, N_NSA_BRANCHES).transpose(0, 2, 3, 1, 4).astype(q.dtype)
    return o_cmp * g[..., 0, None] + o_slc * g[..., 1, None] + o_win * g[..., 2, None]


def forgetting_attention(q, k, v, log_f):
    B, H, L, hd = q.shape
    scale = hd ** -0.5
    c = jnp.cumsum(log_f.astype(jnp.float32), axis=-1)
    nqb = L // Q_BLOCK
    q_blk = q.reshape(B, H, nqb, Q_BLOCK, hd).transpose(2, 0, 1, 3, 4)
    c_blk = c.reshape(B, H, nqb, Q_BLOCK).transpose(2, 0, 1, 3)
    t_blk = jnp.arange(L).reshape(nqb, Q_BLOCK)
    s_pos = jnp.arange(L)

    def block(args):
        qb, cb, tb = args
        s = (jnp.einsum('bhqd,bhkd->bhqk', qb, k).astype(jnp.float32) * scale
             + cb[..., None] - c[:, :, None, :])
        s = jnp.where(s_pos[None, :] <= tb[:, None], s, NEG_INF)
        p = jax.nn.softmax(s, axis=-1)
        return jnp.einsum('bhqk,bhkd->bhqd', p.astype(v.dtype), v)

    o = lax.map(block, (q_blk, c_blk, t_blk))
    return o.transpose(1, 2, 0, 3, 4).reshape(B, H, L, hd)


def hybrid_layer(x, norm_mix, w_in, b_forget, cmp_pe_k, cmp_w1_k, cmp_w2_k,
                 cmp_pe_v, cmp_w1_v, cmp_w2_v, head_norm, w_out, norm_ffn,
                 w_gate, w_up, w_down, slopes_dil, slopes_nsa):
    B, L, _ = x.shape
    h = rms_norm(x, norm_mix) @ w_in
    split_at = tuple(np.cumsum(IN_SPLITS)[:-1].tolist())
    (qa, ka, va, qb, kcb, vcb, ksb, vsb, kwb, vwb, gb,
     qc, kc, vc, fc) = jnp.split(h, split_at, axis=-1)

    def heads(a, n):
        return a.reshape(B, L, n, HEAD_DIM).transpose(0, 2, 1, 3)

    o_a = dilated_attention(heads(qa, H_DIL), heads(ka, H_DIL), heads(va, H_DIL), slopes_dil)

    q_nsa = qb.reshape(B, L, NSA_KV_GROUPS, NSA_REP, HEAD_DIM).transpose(0, 2, 3, 1, 4)
    k_cmp = compress_blocks(heads(kcb, NSA_KV_GROUPS), cmp_pe_k, cmp_w1_k, cmp_w2_k)
    v_cmp = compress_blocks(heads(vcb, NSA_KV_GROUPS), cmp_pe_v, cmp_w1_v, cmp_w2_v)
    gates = jax.nn.sigmoid(gb.astype(jnp.float32)).reshape(B, L, H_NSA, N_NSA_BRANCHES)
    o_b = native_sparse_attention(q_nsa, k_cmp, v_cmp,
                                  heads(ksb, NSA_KV_GROUPS), heads(vsb, NSA_KV_GROUPS),
                                  heads(kwb, NSA_KV_GROUPS), heads(vwb, NSA_KV_GROUPS),
                                  gates, slopes_nsa).reshape(B, H_NSA, L, HEAD_DIM)

    log_f = jax.nn.log_sigmoid((fc + b_forget).astype(jnp.float32)).transpose(0, 2, 1)
    o_c = forgetting_attention(heads(qc, H_FOX), heads(kc, H_FOX), heads(vc, H_FOX), log_f)

    o = jnp.concatenate([o_a, o_b, o_c], axis=1).transpose(0, 2, 1, 3)
    o = rms_norm(o, head_norm.reshape(N_HEADS, HEAD_DIM)).reshape(B, L, D_MODEL)
    x = x + o @ w_out

    hf = rms_norm(x, norm_ffn)
    return x + (jax.nn.silu(hf @ w_gate) * (hf @ w_up)) @ w_down


def setup_inputs(seed: int = 0) -> dict:
    key = jax.random.key(seed)
    ks = jax.random.split(key, 20)
    f32 = jnp.float32

    def nrm(k, shape, scale):
        return jax.random.normal(k, shape, f32) * scale

    def gain(k, shape):
        return 1.0 + 0.02 * jax.random.normal(k, shape, f32)

    return {
        'x': nrm(ks[0], (BATCH, SEQ, D_MODEL), 1.0),
        'norm_mix': gain(ks[1], (DEPTH, D_MODEL)),
        'w_in': nrm(ks[2], (DEPTH, D_MODEL, D_IN), D_MODEL ** -0.5),
        'b_forget': jax.random.uniform(ks[3], (DEPTH, H_FOX), f32, 1.0, 3.0),
        'cmp_pe_k': nrm(ks[4], (DEPTH, CMP_BLOCK, HEAD_DIM), 0.02),
        'cmp_w1_k': nrm(ks[5], (DEPTH, CMP_BLOCK * HEAD_DIM, CMP_HIDDEN), (CMP_BLOCK * HEAD_DIM) ** -0.5),
        'cmp_w2_k': nrm(ks[6], (DEPTH, CMP_HIDDEN, HEAD_DIM), CMP_HIDDEN ** -0.5),
        'cmp_pe_v': nrm(ks[7], (DEPTH, CMP_BLOCK, HEAD_DIM), 0.02),
        'cmp_w1_v': nrm(ks[8], (DEPTH, CMP_BLOCK * HEAD_DIM, CMP_HIDDEN), (CMP_BLOCK * HEAD_DIM) ** -0.5),
        'cmp_w2_v': nrm(ks[9], (DEPTH, CMP_HIDDEN, HEAD_DIM), CMP_HIDDEN ** -0.5),
        'head_norm': gain(ks[10], (DEPTH, D_MODEL)),
        'w_out': nrm(ks[11], (DEPTH, D_MODEL, D_MODEL), D_MODEL ** -0.5),
        'norm_ffn': gain(ks[12], (DEPTH, D_MODEL)),
        'w_gate': nrm(ks[13], (DEPTH, D_MODEL, D_FF), D_MODEL ** -0.5),
        'w_up': nrm(ks[14], (DEPTH, D_MODEL, D_FF), D_MODEL ** -0.5),
        'w_down': nrm(ks[15], (DEPTH, D_FF, D_MODEL), D_FF ** -0.5),
        'norm_final': gain(ks[16], (D_MODEL,)),
    }


def reference(x, norm_mix, w_in, b_forget, cmp_pe_k, cmp_w1_k, cmp_w2_k,
              cmp_pe_v, cmp_w1_v, cmp_w2_v, head_norm, w_out, norm_ffn,
              w_gate, w_up, w_down, norm_final):
    slopes_dil, slopes_nsa = alibi_slopes()
    for l in range(DEPTH):
        x = hybrid_layer(x, norm_mix[l], w_in[l], b_forget[l],
                         cmp_pe_k[l], cmp_w1_k[l], cmp_w2_k[l],
                         cmp_pe_v[l], cmp_w1_v[l], cmp_w2_v[l],
                         head_norm[l], w_out[l], norm_ffn[l],
                         w_gate[l], w_up[l], w_down[l], slopes_dil, slopes_nsa)
    return rms_norm(x, norm_final)
```

```python
import functools

import numpy as np
import jax
import jax.numpy as jnp
from jax import lax
from jax.experimental import pallas as pl
from jax.experimental.pallas import tpu as pltpu

F32 = jnp.float32
BF16 = jnp.bfloat16

HEAD_DIM = 128
LANES = 128
NSA_KV_GROUPS = 2
DILATED_PATTERNS = ((128, 1), (512, 4), (2048, 16))
BAND = 128
CMP_BLOCK = 32
CMP_STRIDE = 16
SEL_BLOCK = 64
N_SELECT = 16
NSA_WINDOW = 512
N_NSA_BRANCHES = 3
FORCE_SCORE = 1e9
NEG_INF = -1e30
NORM_EPS = 1e-6
SCALE = HEAD_DIM ** -0.5
VMEM_LIMIT = 52 * 1024 * 1024

_NT = (((1,), (1,)), ((), ()))


def _params(*sem):
    return pltpu.CompilerParams(dimension_semantics=sem, vmem_limit_bytes=VMEM_LIMIT)


def _pick(n, prefs):
    for t in prefs:
        if n % t == 0:
            return t
    return n


def _rmsnorm_kernel(x_ref, g_ref, o_ref):
    x = x_ref[...].astype(F32)
    ms = jnp.mean(x * x, axis=-1, keepdims=True)
    o_ref[...] = (x * lax.rsqrt(ms + NORM_EPS) * g_ref[...]).astype(o_ref.dtype)


def rmsnorm(x, g, out_dtype):
    m, d = x.shape
    tm = _pick(m, (256, 128, 8))
    return pl.pallas_call(
        _rmsnorm_kernel,
        grid=(m // tm,),
        in_specs=[pl.BlockSpec((tm, d), lambda i: (i, 0)),
                  pl.BlockSpec((1, d), lambda i: (0, 0))],
        out_specs=pl.BlockSpec((tm, d), lambda i: (i, 0)),
        out_shape=jax.ShapeDtypeStruct((m, d), out_dtype),
        compiler_params=_params("parallel"),
        name="rmsnorm",
    )(x, g.reshape(1, d).astype(F32))


def _mm_kernel(a_ref, b_ref, o_ref):
    o_ref[...] = jnp.dot(a_ref[...], b_ref[...], preferred_element_type=F32).astype(o_ref.dtype)


def _mm_res_kernel(a_ref, b_ref, r_ref, o_ref):
    acc = jnp.dot(a_ref[...], b_ref[...], preferred_element_type=F32)
    o_ref[...] = (r_ref[...] + acc).astype(o_ref.dtype)


def _swiglu_kernel(a_ref, wg_ref, wu_ref, o_ref):
    a = a_ref[...]
    g = jnp.dot(a, wg_ref[...], preferred_element_type=F32)
    u = jnp.dot(a, wu_ref[...], preferred_element_type=F32)
    o_ref[...] = (g * (1.0 / (1.0 + jnp.exp(-g))) * u).astype(o_ref.dtype)


def _mm_tiles(m, n, k):
    tn_prefs = (512, 256, 128) if k <= 8192 else (256, 128)
    return _pick(m, (512, 256, 128, 8)), _pick(n, tn_prefs)


def matmul(a, b, out_dtype, residual=None, name="matmul"):
    m, k = a.shape
    n = b.shape[1]
    tm, tn = _mm_tiles(m, n, k)
    in_specs = [pl.BlockSpec((tm, k), lambda i, j: (i, 0)),
                pl.BlockSpec((k, tn), lambda i, j: (0, j))]
    args = [a, b]
    kern = _mm_kernel
    if residual is not None:
        in_specs.append(pl.BlockSpec((tm, tn), lambda i, j: (i, j)))
        args.append(residual)
        kern = _mm_res_kernel
    return pl.pallas_call(
        kern,
        grid=(m // tm, n // tn),
        in_specs=in_specs,
        out_specs=pl.BlockSpec((tm, tn), lambda i, j: (i, j)),
        out_shape=jax.ShapeDtypeStruct((m, n), out_dtype),
        compiler_params=_params("parallel", "arbitrary"),
        name=name,
    )(*args)


def swiglu(a, wg, wu):
    m, k = a.shape
    n = wg.shape[1]
    tm, tn = _mm_tiles(m, n, k)
    tn = min(tn, 256)
    return pl.pallas_call(
        _swiglu_kernel,
        grid=(m // tm, n // tn),
        in_specs=[pl.BlockSpec((tm, k), lambda i, j: (i, 0)),
                  pl.BlockSpec((k, tn), lambda i, j: (0, j)),
                  pl.BlockSpec((k, tn), lambda i, j: (0, j))],
        out_specs=pl.BlockSpec((tm, tn), lambda i, j: (i, j)),
        out_shape=jax.ShapeDtypeStruct((m, n), BF16),
        compiler_params=_params("parallel", "arbitrary"),
        name="swiglu",
    )(a, wg, wu)


def _dil_kernel(q_ref, kp_ref, kc_ref, vp_ref, vc_ref, o_ref, lse_ref, *, dil, slopes):
    i = pl.program_id(2)
    qi = lax.broadcasted_iota(jnp.int32, (BAND, BAND), 0)
    kj = lax.broadcasted_iota(jnp.int32, (BAND, BAND), 1)
    d_cur = qi - kj
    d_prev = d_cur + BAND
    ok_cur = d_cur >= 0
    ok_prev = d_prev <= jnp.where(i > 0, BAND, -1)
    pen_cur = (d_cur * dil).astype(F32)
    pen_prev = (d_prev * dil).astype(F32)
    lane = lax.broadcasted_iota(jnp.int32, (BAND, LANES), 1)
    lse_tile = jnp.zeros((BAND, LANES), F32)
    for h, slope in enumerate(slopes):
        sl = slice(h * HEAD_DIM, (h + 1) * HEAD_DIM)
        q = q_ref[:, sl]
        s_c = lax.dot_general(q, kc_ref[:, sl], _NT, preferred_element_type=F32) * SCALE - slope * pen_cur
        s_p = lax.dot_general(q, kp_ref[:, sl], _NT, preferred_element_type=F32) * SCALE - slope * pen_prev
        s_c = jnp.where(ok_cur, s_c, NEG_INF)
        s_p = jnp.where(ok_prev, s_p, NEG_INF)
        m = jnp.maximum(jnp.max(s_c, axis=-1, keepdims=True), jnp.max(s_p, axis=-1, keepdims=True))
        e_c = jnp.exp(s_c - m)
        e_p = jnp.exp(s_p - m)
        den = jnp.sum(e_c, axis=-1, keepdims=True) + jnp.sum(e_p, axis=-1, keepdims=True)
        inv = 1.0 / den
        o = (jnp.dot((e_c * inv).astype(BF16), vc_ref[:, sl], preferred_element_type=F32)
             + jnp.dot((e_p * inv).astype(BF16), vp_ref[:, sl], preferred_element_type=F32))
        o_ref[:, sl] = o
        lse_tile = jnp.where(lane == h, m + jnp.log(den), lse_tile)
    lse_ref[...] = lse_tile


def dilated_pattern(h_a, batch, seq, dil, slopes):
    m, three_ha = h_a.shape
    ha = three_ha // 3
    nqb = seq // dil // BAND
    view = h_a.reshape(m // dil, dil * three_ha)

    def q_map(b, r, i):
        return (b * nqb + i, 3 * r)

    def kc_map(b, r, i):
        return (b * nqb + i, 3 * r + 1)

    def kp_map(b, r, i):
        return (b * nqb + jnp.maximum(i - 1, 0), 3 * r + 1)

    def vc_map(b, r, i):
        return (b * nqb + i, 3 * r + 2)

    def vp_map(b, r, i):
        return (b * nqb + jnp.maximum(i - 1, 0), 3 * r + 2)

    def o_map(b, r, i):
        return (b * nqb + i, r)

    blk = (BAND, ha)
    o, lse = pl.pallas_call(
        functools.partial(_dil_kernel, dil=dil, slopes=slopes),
        grid=(batch, dil, nqb),
        in_specs=[pl.BlockSpec(blk, q_map), pl.BlockSpec(blk, kp_map), pl.BlockSpec(blk, kc_map),
                  pl.BlockSpec(blk, vp_map), pl.BlockSpec(blk, vc_map)],
        out_specs=[pl.BlockSpec(blk, o_map), pl.BlockSpec((BAND, LANES), o_map)],
        out_shape=[jax.ShapeDtypeStruct((m // dil, dil * ha), F32),
                   jax.ShapeDtypeStruct((m // dil, dil * LANES), F32)],
        compiler_params=_params("parallel", "parallel", "arbitrary"),
        name=f"dilated_d{dil}",
    )(view, view, view, view, view)
    return o.reshape(m, ha), lse.reshape(m, LANES)


def _head_rmsnorm(o, g):
    ms = jnp.mean(o * o, axis=-1, keepdims=True)
    return o * lax.rsqrt(ms + NORM_EPS) * g


def _dil_combine_kernel(o1_ref, o2_ref, o3_ref, l1_ref, l2_ref, l3_ref, g_ref, out_ref, *, n_heads):
    l1, l2, l3 = l1_ref[...], l2_ref[...], l3_ref[...]
    mx = jnp.maximum(jnp.maximum(l1, l2), l3)
    e1, e2, e3 = jnp.exp(l1 - mx), jnp.exp(l2 - mx), jnp.exp(l3 - mx)
    inv = 1.0 / (e1 + e2 + e3)
    w1, w2, w3 = e1 * inv, e2 * inv, e3 * inv
    for h in range(n_heads):
        sl = slice(h * HEAD_DIM, (h + 1) * HEAD_DIM)
        o = (w1[:, h:h + 1] * o1_ref[:, sl] + w2[:, h:h + 1] * o2_ref[:, sl]
             + w3[:, h:h + 1] * o3_ref[:, sl])
        out_ref[:, sl] = _head_rmsnorm(o, g_ref[:, sl]).astype(out_ref.dtype)


def dilated_combine(outs, lses, gain):
    m, ha = outs[0].shape
    tm = _pick(m, (256, 128))
    o_spec = pl.BlockSpec((tm, ha), lambda i: (i, 0))
    l_spec = pl.BlockSpec((tm, LANES), lambda i: (i, 0))
    return pl.pallas_call(
        functools.partial(_dil_combine_kernel, n_heads=ha // HEAD_DIM),
        grid=(m // tm,),
        in_specs=[o_spec, o_spec, o_spec, l_spec, l_spec, l_spec,
                  pl.BlockSpec((1, ha), lambda i: (0, 0))],
        out_specs=o_spec,
        out_shape=jax.ShapeDtypeStruct((m, ha), BF16),
        compiler_params=_params("parallel"),
        name="dilated_combine",
    )(*outs, *lses, gain.reshape(1, ha))


def _compress_kernel(x_ref, pe_ref, w1_ref, w2_ref, o_ref):
    x = (x_ref[...].astype(F32) + pe_ref[...]).astype(BF16)
    hid = jnp.dot(x, w1_ref[...], preferred_element_type=F32)
    act = 0.5 * hid * (1.0 + jnp.tanh(np.sqrt(2.0 / np.pi) * (hid + 0.044715 * (hid * hid * hid))))
    o_ref[...] = jnp.dot(act.astype(BF16), w2_ref[...], preferred_element_type=F32).astype(o_ref.dtype)


def compress(blocks, pe, w1, w2):
    _, rows, width = blocks.shape
    tr = _pick(rows, (256, 128, 8))
    return pl.pallas_call(
        _compress_kernel,
        grid=(2, rows // tr),
        in_specs=[pl.BlockSpec((None, tr, width), lambda s, i: (s, i, 0)),
                  pl.BlockSpec((None, 1, width), lambda s, i: (s, 0, 0)),
                  pl.BlockSpec((None, width, HEAD_DIM), lambda s, i: (s, 0, 0)),
                  pl.BlockSpec((None, HEAD_DIM, HEAD_DIM), lambda s, i: (s, 0, 0))],
        out_specs=pl.BlockSpec((None, tr, HEAD_DIM), lambda s, i: (s, i, 0)),
        out_shape=jax.ShapeDtypeStruct((2, rows, HEAD_DIM), BF16),
        compiler_params=_params("parallel", "parallel"),
        name="nsa_compress",
    )(blocks, pe, w1, w2)


def _nsa_cmp_kernel(q_ref, kc_ref, vc_ref, ov_ref, o_ref, sel_ref, *, tq, n_cmp, n_sel, slopes):
    g = pl.program_id(1)
    i = pl.program_id(2)
    rep = len(slopes[0])
    t = i * tq + lax.broadcasted_iota(jnp.int32, (tq, n_cmp), 0)
    c_end = lax.broadcasted_iota(jnp.int32, (tq, n_cmp), 1) * CMP_STRIDE + (CMP_BLOCK - 1)
    c_dist = t - c_end
    valid = c_dist >= 0
    c_dist_f = c_dist.astype(F32)
    any_valid = jnp.where(i * tq + lax.broadcasted_iota(jnp.int32, (tq, 1), 0) >= CMP_BLOCK - 1, 1.0, 0.0)
    kc = kc_ref[...]
    vc = vc_ref[...]
    p_sum = jnp.zeros((tq, n_cmp), F32)
    for r in range(rep):
        slope = jnp.where(g == 0, slopes[0][r], slopes[1][r])
        sl = slice(r * HEAD_DIM, (r + 1) * HEAD_DIM)
        s = lax.dot_general(q_ref[:, sl], kc, _NT, preferred_element_type=F32) * SCALE - slope * c_dist_f
        s = jnp.where(valid, s, NEG_INF)
        e = jnp.exp(s - jnp.max(s, axis=-1, keepdims=True))
        p = e * (1.0 / jnp.sum(e, axis=-1, keepdims=True)) * any_valid
        o_ref[:, sl] = jnp.dot(p.astype(BF16), vc, preferred_element_type=F32)
        p_sum = p_sum + p
    imp = jnp.dot(p_sum.astype(BF16), ov_ref[...], preferred_element_type=F32)
    blk = lax.broadcasted_iota(jnp.int32, (tq, n_sel), 1)
    cur = lax.shift_right_logical(i * tq + lax.broadcasted_iota(jnp.int32, (tq, n_sel), 0),
                                  int(np.log2(SEL_BLOCK)))
    forced = (blk == 0) | (blk == cur) | (blk == cur - 1)
    imp = jnp.where(forced, FORCE_SCORE, jnp.where(blk <= cur, imp, -1.0))
    rank = jnp.zeros((tq, n_sel), jnp.int32)
    for c in range(n_sel):
        col = imp[:, c:c + 1]
        ahead = (col > imp) | ((col == imp) & (blk > c))
        rank = rank + jnp.where(ahead, 1, 0)
    sel_ref[...] = jnp.where(rank < min(N_SELECT, n_sel), 1.0, 0.0).astype(sel_ref.dtype)


def nsa_compressed(h_b, kv_cmp, overlap, batch, seq, slopes):
    m = h_b.shape[0]
    gw = len(slopes[0]) * HEAD_DIM
    n_cmp = seq // CMP_STRIDE
    n_sel = seq // SEL_BLOCK
    tq = 128
    nq = seq // tq
    return pl.pallas_call(
        functools.partial(_nsa_cmp_kernel, tq=tq, n_cmp=n_cmp, n_sel=n_sel, slopes=slopes),
        grid=(batch, NSA_KV_GROUPS, nq),
        in_specs=[pl.BlockSpec((tq, gw), lambda b, g, i: (b * nq + i, g)),
                  pl.BlockSpec((None, n_cmp, HEAD_DIM), lambda b, g, i: (0, b * NSA_KV_GROUPS + g, 0)),
                  pl.BlockSpec((None, n_cmp, HEAD_DIM), lambda b, g, i: (1, b * NSA_KV_GROUPS + g, 0)),
                  pl.BlockSpec((n_cmp, n_sel), lambda b, g, i: (0, 0))],
        out_specs=[pl.BlockSpec((tq, gw), lambda b, g, i: (b * nq + i, g)),
                   pl.BlockSpec((None, None, tq, n_sel), lambda b, g, i: (b, g, i, 0))],
        out_shape=[jax.ShapeDtypeStruct((m, NSA_KV_GROUPS * gw), F32),
                   jax.ShapeDtypeStruct((batch, NSA_KV_GROUPS, seq, n_sel), BF16)],
        compiler_params=_params("parallel", "parallel", "arbitrary"),
        name="nsa_compressed_select",
    )(h_b, kv_cmp, kv_cmp, overlap)


def _gqa_kernel(*refs, tq, tk, n_steps, window, use_sel, slopes):
    if use_sel:
        q_ref, k_ref, v_ref, sel_ref, o_ref, q_sc, sel_sc, m_sc, l_sc, acc_sc = refs
    else:
        q_ref, k_ref, v_ref, o_ref, q_sc, m_sc, l_sc, acc_sc = refs
    g = pl.program_id(1)
    i = pl.program_id(2)
    step = pl.program_id(3)
    rep = len(slopes[0])
    rows = rep * tq
    j = step if window is None else i * (tq // tk) - (n_steps - 1) + step
    last = i * (tq // tk) if window is None else n_steps - 1

    @pl.when(step == 0)
    def _():
        for r in range(rep):
            q_sc[r * tq:(r + 1) * tq, :] = q_ref[:, r * HEAD_DIM:(r + 1) * HEAD_DIM]
            if use_sel:
                sel_sc[r * tq:(r + 1) * tq, :] = sel_ref[...]
        m_sc[...] = jnp.full(m_sc.shape, NEG_INF, F32)
        l_sc[...] = jnp.zeros(l_sc.shape, F32)
        acc_sc[...] = jnp.zeros(acc_sc.shape, F32)

    active = (j <= last) if window is None else (j >= 0)

    @pl.when(active)
    def _():
        row = lax.broadcasted_iota(jnp.int32, (rows, tk), 0)
        t = i * tq + (row & (tq - 1))
        dist = t - (j * tk + lax.broadcasted_iota(jnp.int32, (rows, tk), 1))
        row1 = lax.broadcasted_iota(jnp.int32, (rows, 1), 0)
        slope = jnp.zeros((rows, 1), F32)
        for r in range(rep):
            s_r = jnp.where(g == 0, slopes[0][r], slopes[1][r])
            slope = jnp.where(row1 >= r * tq, s_r, slope)
        s = lax.dot_general(q_sc[...], k_ref[...], _NT, preferred_element_type=F32) * SCALE
        s = s - slope * dist.astype(F32)
        ok = dist >= 0
        if window is not None:
            ok = ok & (dist <= window)
        if use_sel:
            n_sel = sel_sc.shape[1]
            kblk = lax.shift_right_logical(
                j * tk + lax.broadcasted_iota(jnp.int32, (n_sel, tk), 1), int(np.log2(SEL_BLOCK)))
            expand = jnp.where(lax.broadcasted_iota(jnp.int32, (n_sel, tk), 0) == kblk, 1.0, 0.0).astype(BF16)
            picked = jnp.dot(sel_sc[...], expand, preferred_element_type=F32)
            ok = ok & (picked > 0.5)
        s = jnp.where(ok, s, NEG_INF)
        m_prev = m_sc[:, 0:1]
        m_new = jnp.maximum(m_prev, jnp.max(s, axis=-1, keepdims=True))
        alpha = jnp.exp(m_prev - m_new)
        p = jnp.exp(s - m_new)
        l_new = alpha * l_sc[:, 0:1] + jnp.sum(p, axis=-1, keepdims=True)
        acc_sc[...] = alpha * acc_sc[...] + jnp.dot(p.astype(BF16), v_ref[...], preferred_element_type=F32)
        m_sc[...] = jnp.broadcast_to(m_new, m_sc.shape)
        l_sc[...] = jnp.broadcast_to(l_new, l_sc.shape)

    @pl.when(step == n_steps - 1)
    def _():
        o = acc_sc[...] * (1.0 / l_sc[:, 0:1])
        for r in range(rep):
            o_ref[:, r * HEAD_DIM:(r + 1) * HEAD_DIM] = o[r * tq:(r + 1) * tq, :]


def nsa_branch(h_b, k_col, v_col, sel, batch, seq, slopes, window):
    m = h_b.shape[0]
    rep = len(slopes[0])
    gw = rep * HEAD_DIM
    tq = tk = 128
    nq = seq // tq
    use_sel = sel is not None
    n_steps = seq // tk if window is None else window // tk + 1

    def kv_block(i, step):
        if window is None:
            return jnp.minimum(step, i)
        return jnp.maximum(i - (n_steps - 1) + step, 0)

    in_specs = [pl.BlockSpec((tq, gw), lambda b, g, i, s: (b * nq + i, g)),
                pl.BlockSpec((tk, HEAD_DIM), lambda b, g, i, s: (b * nq + kv_block(i, s), k_col + g)),
                pl.BlockSpec((tk, HEAD_DIM), lambda b, g, i, s: (b * nq + kv_block(i, s), v_col + g))]
    args = [h_b, h_b, h_b]
    scratch = [pltpu.VMEM((rep * tq, HEAD_DIM), BF16)]
    if use_sel:
        n_sel = sel.shape[-1]
        in_specs.append(pl.BlockSpec((None, None, tq, n_sel), lambda b, g, i, s: (b, g, i, 0)))
        args.append(sel)
        scratch.append(pltpu.VMEM((rep * tq, n_sel), BF16))
    scratch += [pltpu.VMEM((rep * tq, LANES), F32), pltpu.VMEM((rep * tq, LANES), F32),
                pltpu.VMEM((rep * tq, HEAD_DIM), F32)]
    return pl.pallas_call(
        functools.partial(_gqa_kernel, tq=tq, tk=tk, n_steps=n_steps, window=window,
                          use_sel=use_sel, slopes=slopes),
        grid=(batch, NSA_KV_GROUPS, nq, n_steps),
        in_specs=in_specs,
        out_specs=pl.BlockSpec((tq, gw), lambda b, g, i, s: (b * nq + i, g)),
        out_shape=jax.ShapeDtypeStruct((m, NSA_KV_GROUPS * gw), F32),
        scratch_shapes=scratch,
        compiler_params=_params("parallel", "parallel", "parallel", "arbitrary"),
        name="nsa_selected" if use_sel else "nsa_window",
    )(*args)


def _nsa_combine_kernel(oc_ref, os_ref, ow_ref, z_ref, g_ref, out_ref, *, n_heads):
    gate = 1.0 / (1.0 + jnp.exp(-z_ref[...]))
    for h in range(n_heads):
        sl = slice(h * HEAD_DIM, (h + 1) * HEAD_DIM)
        c = h * N_NSA_BRANCHES
        o = (oc_ref[:, sl] * gate[:, c:c + 1] + os_ref[:, sl] * gate[:, c + 1:c + 2]
             + ow_ref[:, sl] * gate[:, c + 2:c + 3])
        out_ref[:, sl] = _head_rmsnorm(o, g_ref[:, sl]).astype(out_ref.dtype)


def nsa_combine(o_cmp, o_slc, o_win, h_small, gain):
    m, hb = o_cmp.shape
    tm = _pick(m, (256, 128))
    o_spec = pl.BlockSpec((tm, hb), lambda i: (i, 0))
    return pl.pallas_call(
        functools.partial(_nsa_combine_kernel, n_heads=hb // HEAD_DIM),
        grid=(m // tm,),
        in_specs=[o_spec, o_spec, o_spec, pl.BlockSpec((tm, LANES), lambda i: (i, 0)),
                  pl.BlockSpec((1, hb), lambda i: (0, 0))],
        out_specs=o_spec,
        out_shape=jax.ShapeDtypeStruct((m, hb), BF16),
        compiler_params=_params("parallel"),
        name="nsa_combine",
    )(o_cmp, o_slc, o_win, h_small, gain.reshape(1, hb))


def _forget_cumsum_kernel(z_ref, b_ref, c_ref, carry_sc, *, tr):
    @pl.when(pl.program_id(1) == 0)
    def _():
        carry_sc[...] = jnp.zeros(carry_sc.shape, F32)

    z = z_ref[...] + b_ref[...]
    log_f = jnp.minimum(z, 0.0) - jnp.log1p(jnp.exp(-jnp.abs(z)))
    tri = jnp.where(lax.broadcasted_iota(jnp.int32, (tr, tr), 0)
                    >= lax.broadcasted_iota(jnp.int32, (tr, tr), 1), 1.0, 0.0)
    c = jnp.dot(tri, log_f, preferred_element_type=F32, precision=lax.Precision.HIGHEST) + carry_sc[0:1, :]
    c_ref[...] = c
    carry_sc[...] = jnp.broadcast_to(c[tr - 1:tr, :], carry_sc.shape)


def forget_cumsum(h_small, bias_row, batch, seq):
    m = h_small.shape[0]
    tr = _pick(seq, (256, 128))
    nb = seq // tr
    return pl.pallas_call(
        functools.partial(_forget_cumsum_kernel, tr=tr),
        grid=(batch, nb),
        in_specs=[pl.BlockSpec((tr, LANES), lambda b, i: (b * nb + i, 0)),
                  pl.BlockSpec((1, LANES), lambda b, i: (0, 0))],
        out_specs=pl.BlockSpec((tr, LANES), lambda b, i: (b * nb + i, 0)),
        out_shape=jax.ShapeDtypeStruct((m, LANES), F32),
        scratch_shapes=[pltpu.VMEM((8, LANES), F32)],
        compiler_params=_params("parallel", "arbitrary"),
        name="forget_cumsum",
    )(h_small, bias_row)


def _fox_kernel(q_ref, k_ref, v_ref, cq_ref, ck_ref, g_ref, o_ref, m_sc, l_sc, acc_sc,
                *, tq, tk, n_heads, col0):
    i = pl.program_id(1)
    j = pl.program_id(2)

    @pl.when(j == 0)
    def _():
        m_sc[...] = jnp.full(m_sc.shape, NEG_INF, F32)
        l_sc[...] = jnp.zeros(l_sc.shape, F32)
        acc_sc[...] = jnp.zeros(acc_sc.shape, F32)

    @pl.when(j <= i)
    def _():
        causal = (j * tk + lax.broadcasted_iota(jnp.int32, (tq, tk), 1)
                  <= i * tq + lax.broadcasted_iota(jnp.int32, (tq, tk), 0))
        cq = cq_ref[...]
        ck = ck_ref[...]
        for h in range(n_heads):
            sl = slice(h * HEAD_DIM, (h + 1) * HEAD_DIM)
            s = lax.dot_general(q_ref[:, sl], k_ref[:, sl], _NT, preferred_element_type=F32) * SCALE
            s = s + cq[:, col0 + h:col0 + h + 1] - ck[h:h + 1, :]
            s = jnp.where(causal, s, NEG_INF)
            m_prev = m_sc[h, :, 0:1]
            m_new = jnp.maximum(m_prev, jnp.max(s, axis=-1, keepdims=True))
            alpha = jnp.exp(m_prev - m_new)
            p = jnp.exp(s - m_new)
            l_new = alpha * l_sc[h, :, 0:1] + jnp.sum(p, axis=-1, keepdims=True)
            acc_sc[h] = alpha * acc_sc[h] + jnp.dot(p.astype(BF16), v_ref[:, sl], preferred_element_type=F32)
            m_sc[h] = jnp.broadcast_to(m_new, (tq, LANES))
            l_sc[h] = jnp.broadcast_to(l_new, (tq, LANES))

    @pl.when(j == i)
    def _():
        for h in range(n_heads):
            sl = slice(h * HEAD_DIM, (h + 1) * HEAD_DIM)
            o = acc_sc[h] * (1.0 / l_sc[h, :, 0:1])
            o_ref[:, sl] = _head_rmsnorm(o, g_ref[:, sl]).astype(o_ref.dtype)


def forgetting_attention(h_c, c_tok, c_head, gain, batch, seq, col0):
    m, three_hc = h_c.shape
    hc = three_hc // 3
    n_heads = hc // HEAD_DIM
    hp = c_head.shape[0] // batch
    tq = tk = _pick(seq, (256, 128))
    nq = seq // tq
    return pl.pallas_call(
        functools.partial(_fox_kernel, tq=tq, tk=tk, n_heads=n_heads, col0=col0),
        grid=(batch, nq, nq),
        in_specs=[pl.BlockSpec((tq, hc), lambda b, i, j: (b * nq + i, 0)),
                  pl.BlockSpec((tk, hc), lambda b, i, j: (b * nq + jnp.minimum(j, i), 1)),
                  pl.BlockSpec((tk, hc), lambda b, i, j: (b * nq + jnp.minimum(j, i), 2)),
                  pl.BlockSpec((tq, LANES), lambda b, i, j: (b * nq + i, 0)),
                  pl.BlockSpec((hp, tk), lambda b, i, j: (b, jnp.minimum(j, i))),
                  pl.BlockSpec((1, hc), lambda b, i, j: (0, 0))],
        out_specs=pl.BlockSpec((tq, hc), lambda b, i, j: (b * nq + i, 0)),
        out_shape=jax.ShapeDtypeStruct((m, hc), BF16),
        scratch_shapes=[pltpu.VMEM((n_heads, tq, LANES), F32), pltpu.VMEM((n_heads, tq, LANES), F32),
                        pltpu.VMEM((n_heads, tq, HEAD_DIM), F32)],
        compiler_params=_params("parallel", "parallel", "arbitrary"),
        name="forgetting_attention",
    )(h_c, h_c, h_c, c_tok, c_head, gain.reshape(1, hc))


def _alibi_slopes(h_dil, h_nsa):
    n = h_dil + h_nsa
    s = (2.0 ** (-8.0 * np.arange(1, n + 1) / n)).astype(np.float32)
    nsa_mask = np.isin(np.arange(n) % 5, [1, 3])
    dil = tuple(float(v) for v in s[~nsa_mask])
    nsa = s[nsa_mask].reshape(NSA_KV_GROUPS, h_nsa // NSA_KV_GROUPS)
    return dil, tuple(tuple(float(v) for v in row) for row in nsa)


def _overlap_matrix(seq):
    n_cmp = seq // CMP_STRIDE
    n_sel = seq // SEL_BLOCK
    c_start = np.arange(n_cmp) * CMP_STRIDE
    s_start = np.arange(n_sel) * SEL_BLOCK
    ov = ((c_start[:, None] < s_start[None, :] + SEL_BLOCK)
          & (c_start[:, None] + CMP_BLOCK > s_start[None, :]))
    return jnp.asarray(ov, BF16)


def _layer(x, batch, seq, norm_mix, w_in, b_forget, cmp_pe, cmp_w1, cmp_w2, head_norm, w_out,
           norm_ffn, w_gate, w_up, w_down):
    m, d = x.shape
    n_heads = d // HEAD_DIM
    h_dil = 3 * n_heads // 8
    h_nsa = n_heads // 4
    h_fox = n_heads - h_dil - h_nsa
    rep = h_nsa // NSA_KV_GROUPS
    ha, hb, hc, gkv = h_dil * HEAD_DIM, h_nsa * HEAD_DIM, h_fox * HEAD_DIM, NSA_KV_GROUPS * HEAD_DIM
    n_gate = h_nsa * N_NSA_BRANCHES
    slopes_dil, slopes_nsa = _alibi_slopes(h_dil, h_nsa)

    a0 = 0
    b0 = a0 + 3 * ha
    g0 = b0 + hb + 6 * gkv
    c0 = g0 + n_gate
    f0 = c0 + 3 * hc
    w_a = w_in[:, a0:b0].astype(BF16)
    w_b = w_in[:, b0:g0].astype(BF16)
    w_c = w_in[:, c0:f0].astype(BF16)
    w_small = jnp.concatenate(
        [w_in[:, g0:c0], w_in[:, f0:f0 + h_fox], jnp.zeros((d, LANES - n_gate - h_fox), w_in.dtype)],
        axis=1).astype(BF16)
    bias_row = jnp.zeros((1, LANES), F32).at[0, n_gate:n_gate + h_fox].set(b_forget.astype(F32))

    xn = rmsnorm(x, norm_mix, BF16)
    h_a = matmul(xn, w_a, BF16, name="in_proj_a")
    h_b = matmul(xn, w_b, BF16, name="in_proj_b")
    h_c = matmul(xn, w_c, BF16, name="in_proj_c")
    h_small = matmul(xn, w_small, F32, name="in_proj_small")

    outs, lses = [], []
    for _, dil in DILATED_PATTERNS:
        o, lse = dilated_pattern(h_a, batch, seq, dil, slopes_dil)
        outs.append(o)
        lses.append(lse)
    o_a = dilated_combine(outs, lses, head_norm[:ha])

    n_chunk = seq // CMP_STRIDE

    def cmp_blocks(col):
        a = h_b[:, col:col + gkv].reshape(batch, seq, NSA_KV_GROUPS, HEAD_DIM).transpose(0, 2, 1, 3)
        chunks = a.reshape(batch, NSA_KV_GROUPS, n_chunk, CMP_STRIDE * HEAD_DIM)
        nxt = jnp.concatenate([chunks[:, :, 1:], jnp.zeros_like(chunks[:, :, :1])], axis=2)
        return jnp.concatenate([chunks, nxt], axis=-1).reshape(batch * NSA_KV_GROUPS * n_chunk, -1)

    blocks = jnp.stack([cmp_blocks(hb), cmp_blocks(hb + gkv)])
    kv_cmp = compress(blocks, cmp_pe.reshape(2, 1, CMP_BLOCK * HEAD_DIM).astype(F32),
                      cmp_w1.astype(BF16), cmp_w2.astype(BF16))
    o_cmp, sel = nsa_compressed(h_b, kv_cmp, _overlap_matrix(seq), batch, seq, slopes_nsa)
    col = hb // HEAD_DIM
    o_slc = nsa_branch(h_b, col + 2 * NSA_KV_GROUPS, col + 3 * NSA_KV_GROUPS, sel, batch, seq, slopes_nsa, None)
    o_win = nsa_branch(h_b, col + 4 * NSA_KV_GROUPS, col + 5 * NSA_KV_GROUPS, None, batch, seq, slopes_nsa,
                       NSA_WINDOW)
    o_b = nsa_combine(o_cmp, o_slc, o_win, h_small, head_norm[ha:ha + hb])

    c_tok = forget_cumsum(h_small, bias_row, batch, seq)
    hp = -(-h_fox // 8) * 8
    c_head = c_tok.reshape(batch, seq, LANES)[:, :, n_gate:n_gate + h_fox].transpose(0, 2, 1)
    c_head = jnp.pad(c_head, ((0, 0), (0, hp - h_fox), (0, 0))).reshape(batch * hp, seq)
    o_c = forgetting_attention(h_c, c_tok, c_head, head_norm[ha + hb:], batch, seq, n_gate)

    o = jnp.concatenate([o_a, o_b, o_c], axis=1)
    x = matmul(o, w_out.astype(BF16), F32, residual=x, name="out_proj")

    hf = rmsnorm(x, norm_ffn, BF16)
    gu = swiglu(hf, w_gate.astype(BF16), w_up.astype(BF16))
    return matmul(gu, w_down.astype(BF16), F32, residual=x, name="down_proj")


def kernel(x, norm_mix, w_in, b_forget, cmp_pe_k, cmp_w1_k, cmp_w2_k, cmp_pe_v, cmp_w1_v, cmp_w2_v,
           head_norm, w_out, norm_ffn, w_gate, w_up, w_down, norm_final):
    batch, seq, d = x.shape
    depth = norm_mix.shape[0]
    h = x.reshape(batch * seq, d)
    for l in range(depth):
        h = _layer(h, batch, seq, norm_mix[l], w_in[l], b_forget[l],
                   jnp.stack([cmp_pe_k[l], cmp_pe_v[l]]), jnp.stack([cmp_w1_k[l], cmp_w1_v[l]]),
                   jnp.stack([cmp_w2_k[l], cmp_w2_v[l]]), head_norm[l], w_out[l], norm_ffn[l],
                   w_gate[l], w_up[l], w_down[l])
    return rmsnorm(h, norm_final, x.dtype).reshape(batch, seq, d)
```

```python
import functools

import numpy as np
import jax
import jax.numpy as jnp
from jax import lax
from jax.experimental import pallas as pl
from jax.experimental.pallas import tpu as pltpu

F32 = jnp.float32
BF16 = jnp.bfloat16

HEAD_DIM = 128
LANES = 128
NSA_KV_GROUPS = 2
DILATED_PATTERNS = ((128, 1), (512, 4), (2048, 16))
BAND = 128
CMP_BLOCK = 32
CMP_STRIDE = 16
SEL_BLOCK = 64
N_SELECT = 16
NSA_WINDOW = 512
N_NSA_BRANCHES = 3
FORCE_SCORE = 1e9
NEG_INF = -1e30
NORM_EPS = 1e-6
SCALE = HEAD_DIM ** -0.5
VMEM_LIMIT = 52 * 1024 * 1024
MM_VMEM_BUDGET = 44 * 1024 * 1024

_NT = (((1,), (1,)), ((), ()))


def _params(*sem):
    return pltpu.CompilerParams(dimension_semantics=sem, vmem_limit_bytes=VMEM_LIMIT)


def _pick(n, prefs):
    for t in prefs:
        if n % t == 0:
            return t
    return n


def _rmsnorm_kernel(x_ref, g_ref, o_ref):
    x = x_ref[...].astype(F32)
    ms = jnp.mean(x * x, axis=-1, keepdims=True)
    o_ref[...] = (x * lax.rsqrt(ms + NORM_EPS) * g_ref[...]).astype(o_ref.dtype)


def rmsnorm(x, g, out_dtype):
    m, d = x.shape
    tm = _pick(m, (256, 128, 8))
    return pl.pallas_call(
        _rmsnorm_kernel,
        grid=(m // tm,),
        in_specs=[pl.BlockSpec((tm, d), lambda i: (i, 0)),
                  pl.BlockSpec((1, d), lambda i: (0, 0))],
        out_specs=pl.BlockSpec((tm, d), lambda i: (i, 0)),
        out_shape=jax.ShapeDtypeStruct((m, d), out_dtype),
        compiler_params=_params("parallel"),
        name="rmsnorm",
    )(x, g.reshape(1, d).astype(F32))


def _mm_kernel(a_ref, w_ref, o_ref, wb_sc):
    @pl.when(pl.program_id(1) == 0)
    def _():
        wb_sc[...] = w_ref[...].astype(BF16)

    o_ref[...] = jnp.dot(a_ref[...], wb_sc[...], preferred_element_type=F32).astype(o_ref.dtype)


def _mm_res_kernel(a_ref, w_ref, r_ref, o_ref, wb_sc):
    @pl.when(pl.program_id(1) == 0)
    def _():
        wb_sc[...] = w_ref[...].astype(BF16)

    acc = jnp.dot(a_ref[...], wb_sc[...], preferred_element_type=F32)
    o_ref[...] = (r_ref[...] + acc).astype(o_ref.dtype)


def _swiglu_kernel(a_ref, wg_ref, wu_ref, o_ref, wg_sc, wu_sc):
    @pl.when(pl.program_id(1) == 0)
    def _():
        wg_sc[...] = wg_ref[...].astype(BF16)
        wu_sc[...] = wu_ref[...].astype(BF16)

    a = a_ref[...]
    g = jnp.dot(a, wg_sc[...], preferred_element_type=F32)
    u = jnp.dot(a, wu_sc[...], preferred_element_type=F32)
    o_ref[...] = (g * (1.0 / (1.0 + jnp.exp(-g))) * u).astype(o_ref.dtype)


def _mm_tiles(m, n, k, col0, n_weights, out_bytes):
    best = None
    for tn in (1024, 512, 256, 128):
        if n % tn or col0 % tn:
            continue
        for tm in (1024, 512, 256, 128):
            if m % tm:
                continue
            need = (n_weights * k * tn * (4 + 2) + 2 * tm * k * 2 + 2 * tm * tn * out_bytes
                    + (n_weights + 1) * tm * tn * 4)
            if need <= MM_VMEM_BUDGET and (best is None or tm * tn > best[0] * best[1]):
                best = (tm, tn)
    assert best is not None, (m, n, k, col0)
    return best


def _weight_spec(k, tn, index_map):
    return pl.BlockSpec((None, k, tn), index_map, pipeline_mode=pl.Buffered(1))


def matmul(a, w, layer, col0, n, out_dtype, residual=None, name="matmul"):
    m, k = a.shape
    out_bytes = jnp.dtype(out_dtype).itemsize + (0 if residual is None else residual.dtype.itemsize)
    tm, tn = _mm_tiles(m, n, k, col0, 1, out_bytes)
    cb = col0 // tn
    in_specs = [pl.BlockSpec((tm, k), lambda j, i: (i, 0)),
                _weight_spec(k, tn, lambda j, i: (layer, 0, cb + j))]
    args = [a, w]
    kern = _mm_kernel
    if residual is not None:
        in_specs.append(pl.BlockSpec((tm, tn), lambda j, i: (i, j)))
        args.append(residual)
        kern = _mm_res_kernel
    return pl.pallas_call(
        kern,
        grid=(n // tn, m // tm),
        in_specs=in_specs,
        out_specs=pl.BlockSpec((tm, tn), lambda j, i: (i, j)),
        out_shape=jax.ShapeDtypeStruct((m, n), out_dtype),
        scratch_shapes=[pltpu.VMEM((k, tn), BF16)],
        compiler_params=_params("parallel", "arbitrary"),
        name=name,
    )(*args)


def swiglu(a, wg, wu, layer):
    m, k = a.shape
    n = wg.shape[2]
    tm, tn = _mm_tiles(m, n, k, 0, 2, 2)
    w_spec = _weight_spec(k, tn, lambda j, i: (layer, 0, j))
    return pl.pallas_call(
        _swiglu_kernel,
        grid=(n // tn, m // tm),
        in_specs=[pl.BlockSpec((tm, k), lambda j, i: (i, 0)), w_spec, w_spec],
        out_specs=pl.BlockSpec((tm, tn), lambda j, i: (i, j)),
        out_shape=jax.ShapeDtypeStruct((m, n), BF16),
        scratch_shapes=[pltpu.VMEM((k, tn), BF16), pltpu.VMEM((k, tn), BF16)],
        compiler_params=_params("parallel", "arbitrary"),
        name="swiglu",
    )(a, wg, wu)


def _head_rmsnorm(o, g):
    ms = jnp.mean(o * o, axis=-1, keepdims=True)
    return o * lax.rsqrt(ms + NORM_EPS) * g


def _flash_step(s, v, m_sc, l_sc, acc_sc):
    m_prev = m_sc[...]
    m_new = jnp.maximum(m_prev, jnp.max(s, axis=-1, keepdims=True))
    alpha = jnp.exp(m_prev - m_new)
    p = jnp.exp(s - pltpu.repeat(m_new, s.shape[1] // LANES, axis=1))
    l_sc[...] = alpha * l_sc[...] + jnp.sum(p, axis=-1, keepdims=True)
    acc_sc[...] = alpha * acc_sc[...] + jnp.dot(p.astype(BF16), v, preferred_element_type=F32)
    m_sc[...] = m_new


def _flash_reset(m_sc, l_sc, acc_sc):
    m_sc[...] = jnp.full(m_sc.shape, NEG_INF, F32)
    l_sc[...] = jnp.zeros(l_sc.shape, F32)
    acc_sc[...] = jnp.zeros(acc_sc.shape, F32)


def _dil_kernel(q_ref, kp_ref, kc_ref, vp_ref, vc_ref, o_ref, lse_ref, *, dil, slopes):
    i = pl.program_id(2)
    qi = lax.broadcasted_iota(jnp.int32, (BAND, BAND), 0)
    kj = lax.broadcasted_iota(jnp.int32, (BAND, BAND), 1)
    d_cur = qi - kj
    d_prev = d_cur + BAND
    ok_cur = d_cur >= 0
    ok_prev = d_prev <= jnp.where(i > 0, BAND, -1)
    pen_cur = (d_cur * dil).astype(F32)
    pen_prev = (d_prev * dil).astype(F32)
    lane = lax.broadcasted_iota(jnp.int32, (BAND, LANES), 1)
    lse_tile = jnp.zeros((BAND, LANES), F32)
    for h, slope in enumerate(slopes):
        sl = slice(h * HEAD_DIM, (h + 1) * HEAD_DIM)
        q = q_ref[:, sl]
        s_c = lax.dot_general(q, kc_ref[:, sl], _NT, preferred_element_type=F32) * SCALE - slope * pen_cur
        s_p = lax.dot_general(q, kp_ref[:, sl], _NT, preferred_element_type=F32) * SCALE - slope * pen_prev
        s_c = jnp.where(ok_cur, s_c, NEG_INF)
        s_p = jnp.where(ok_prev, s_p, NEG_INF)
        m = jnp.maximum(jnp.max(s_c, axis=-1, keepdims=True), jnp.max(s_p, axis=-1, keepdims=True))
        e_c = jnp.exp(s_c - m)
        e_p = jnp.exp(s_p - m)
        den = jnp.sum(e_c, axis=-1, keepdims=True) + jnp.sum(e_p, axis=-1, keepdims=True)
        inv = 1.0 / den
        o = (jnp.dot((e_c * inv).astype(BF16), vc_ref[:, sl], preferred_element_type=F32)
             + jnp.dot((e_p * inv).astype(BF16), vp_ref[:, sl], preferred_element_type=F32))
        o_ref[:, sl] = o
        lse_tile = jnp.where(lane == h, m + jnp.log(den), lse_tile)
    lse_ref[...] = lse_tile


def dilated_pattern(h_a, batch, seq, dil, slopes):
    m, three_ha = h_a.shape
    ha = three_ha // 3
    nqb = seq // dil // BAND
    view = h_a.reshape(m // dil, dil * three_ha)

    def q_map(b, r, i):
        return (b * nqb + i, 3 * r)

    def kc_map(b, r, i):
        return (b * nqb + i, 3 * r + 1)

    def kp_map(b, r, i):
        return (b * nqb + jnp.maximum(i - 1, 0), 3 * r + 1)

    def vc_map(b, r, i):
        return (b * nqb + i, 3 * r + 2)

    def vp_map(b, r, i):
        return (b * nqb + jnp.maximum(i - 1, 0), 3 * r + 2)

    def o_map(b, r, i):
        return (b * nqb + i, r)

    blk = (BAND, ha)
    o, lse = pl.pallas_call(
        functools.partial(_dil_kernel, dil=dil, slopes=slopes),
        grid=(batch, dil, nqb),
        in_specs=[pl.BlockSpec(blk, q_map), pl.BlockSpec(blk, kp_map), pl.BlockSpec(blk, kc_map),
                  pl.BlockSpec(blk, vp_map), pl.BlockSpec(blk, vc_map)],
        out_specs=[pl.BlockSpec(blk, o_map), pl.BlockSpec((BAND, LANES), o_map)],
        out_shape=[jax.ShapeDtypeStruct((m // dil, dil * ha), F32),
                   jax.ShapeDtypeStruct((m // dil, dil * LANES), F32)],
        compiler_params=_params("parallel", "parallel", "arbitrary"),
        name=f"dilated_d{dil}",
    )(view, view, view, view, view)
    return o.reshape(m, ha), lse.reshape(m, LANES)


def _dil_combine_kernel(o1_ref, o2_ref, o3_ref, l1_ref, l2_ref, l3_ref, g_ref, out_ref, *, n_heads):
    l1, l2, l3 = l1_ref[...], l2_ref[...], l3_ref[...]
    mx = jnp.maximum(jnp.maximum(l1, l2), l3)
    e1, e2, e3 = jnp.exp(l1 - mx), jnp.exp(l2 - mx), jnp.exp(l3 - mx)
    inv = 1.0 / (e1 + e2 + e3)
    w1, w2, w3 = e1 * inv, e2 * inv, e3 * inv
    for h in range(n_heads):
        sl = slice(h * HEAD_DIM, (h + 1) * HEAD_DIM)
        o = (w1[:, h:h + 1] * o1_ref[:, sl] + w2[:, h:h + 1] * o2_ref[:, sl]
             + w3[:, h:h + 1] * o3_ref[:, sl])
        out_ref[:, sl] = _head_rmsnorm(o, g_ref[:, sl]).astype(out_ref.dtype)


def dilated_combine(outs, lses, gain):
    m, ha = outs[0].shape
    tm = _pick(m, (256, 128))
    o_spec = pl.BlockSpec((tm, ha), lambda i: (i, 0))
    l_spec = pl.BlockSpec((tm, LANES), lambda i: (i, 0))
    return pl.pallas_call(
        functools.partial(_dil_combine_kernel, n_heads=ha // HEAD_DIM),
        grid=(m // tm,),
        in_specs=[o_spec, o_spec, o_spec, l_spec, l_spec, l_spec,
                  pl.BlockSpec((1, ha), lambda i: (0, 0))],
        out_specs=o_spec,
        out_shape=jax.ShapeDtypeStruct((m, ha), BF16),
        compiler_params=_params("parallel"),
        name="dilated_combine",
    )(*outs, *lses, gain.reshape(1, ha))


def _compress_kernel(x_ref, pe_ref, w1_ref, w2_ref, o_ref):
    x = (x_ref[...].astype(F32) + pe_ref[...]).astype(BF16)
    hid = jnp.dot(x, w1_ref[...], preferred_element_type=F32)
    act = 0.5 * hid * (1.0 + jnp.tanh(np.sqrt(2.0 / np.pi) * (hid + 0.044715 * (hid * hid * hid))))
    o_ref[...] = jnp.dot(act.astype(BF16), w2_ref[...], preferred_element_type=F32).astype(o_ref.dtype)


def compress(blocks, pe, w1, w2):
    _, rows, width = blocks.shape
    tr = _pick(rows, (256, 128, 8))
    return pl.pallas_call(
        _compress_kernel,
        grid=(2, rows // tr),
        in_specs=[pl.BlockSpec((None, tr, width), lambda s, i: (s, i, 0)),
                  pl.BlockSpec((None, 1, width), lambda s, i: (s, 0, 0)),
                  pl.BlockSpec((None, width, HEAD_DIM), lambda s, i: (s, 0, 0)),
                  pl.BlockSpec((None, HEAD_DIM, HEAD_DIM), lambda s, i: (s, 0, 0))],
        out_specs=pl.BlockSpec((None, tr, HEAD_DIM), lambda s, i: (s, i, 0)),
        out_shape=jax.ShapeDtypeStruct((2, rows, HEAD_DIM), BF16),
        compiler_params=_params("parallel", "parallel"),
        name="nsa_compress",
    )(blocks, pe, w1, w2)


def _nsa_kernel(q_ref, kc_ref, vc_ref, ks_ref, vs_ref, kw_ref, vw_ref, ov_ref, z_ref, g_ref, o_ref,
                q_sc, sel_sc, m_sc, l_sc, acc_sc, *, tq, tk, n_cmp, n_sel, slopes):
    g = pl.program_id(1)
    i = pl.program_id(2)
    rep = len(slopes[0])
    sel_shift = int(np.log2(SEL_BLOCK))

    for r in range(rep):
        q_sc[r * tq:(r + 1) * tq, :] = q_ref[:, r * HEAD_DIM:(r + 1) * HEAD_DIM]
    head_slopes = [jnp.where(g == 0, slopes[0][r], slopes[1][r]) for r in range(rep)]

    def biased(s, dist, ok):
        dist_f = dist.astype(F32)
        return jnp.concatenate(
            [jnp.where(ok, s[r * tq:(r + 1) * tq] * SCALE - head_slopes[r] * dist_f, NEG_INF)
             for r in range(rep)], axis=0)

    def rel_pos(k0, width):
        return (lax.broadcasted_iota(jnp.int32, (tq, width), 0)
                - lax.broadcasted_iota(jnp.int32, (tq, width), 1)) + (i * tq - k0)

    t_pos = i * tq + lax.broadcasted_iota(jnp.int32, (tq, n_cmp), 0)
    c_dist = t_pos - (lax.broadcasted_iota(jnp.int32, (tq, n_cmp), 1) * CMP_STRIDE + (CMP_BLOCK - 1))
    s = lax.dot_general(q_sc[...], kc_ref[...], _NT, preferred_element_type=F32)
    s = biased(s, c_dist, c_dist >= 0)
    e = jnp.exp(s - jnp.max(s, axis=-1, keepdims=True))
    p = e * (1.0 / jnp.sum(e, axis=-1, keepdims=True))
    o_cmp = jnp.dot(p.astype(BF16), vc_ref[...], preferred_element_type=F32)
    any_valid = jnp.where(i * tq + lax.broadcasted_iota(jnp.int32, (tq, 1), 0) >= CMP_BLOCK - 1, 1.0, 0.0)

    p_sum = p[0:tq]
    for r in range(1, rep):
        p_sum = p_sum + p[r * tq:(r + 1) * tq]
    p_sum = p_sum * any_valid
    imp = jnp.dot(p_sum.astype(BF16), ov_ref[...], preferred_element_type=F32)
    blk = lax.broadcasted_iota(jnp.int32, (tq, n_sel), 1)
    cur = lax.shift_right_logical(i * tq + lax.broadcasted_iota(jnp.int32, (tq, n_sel), 0), sel_shift)
    forced = (blk == 0) | (blk == cur) | (blk == cur - 1)
    imp = jnp.where(forced, FORCE_SCORE, jnp.where(blk <= cur, imp, -1.0))
    rank = jnp.zeros((tq, n_sel), jnp.int32)
    for c in range(n_sel):
        col = imp[:, c:c + 1]
        ahead = (col > imp) | ((col == imp) & (blk > c))
        rank = rank + jnp.where(ahead, 1, 0)
    sel_sc[...] = jnp.where(rank < min(N_SELECT, n_sel), 1.0, 0.0).astype(BF16)

    _flash_reset(m_sc, l_sc, acc_sc)

    def slc_block(kb, carry):
        k0 = pl.multiple_of(kb * tk, tk)
        dist = rel_pos(k0, tk)
        kblk = lax.shift_right_logical(k0 + lax.broadcasted_iota(jnp.int32, (n_sel, tk), 1), sel_shift)
        expand = jnp.where(lax.broadcasted_iota(jnp.int32, (n_sel, tk), 0) == kblk, 1.0, 0.0).astype(BF16)
        picked = jnp.dot(sel_sc[...], expand, preferred_element_type=F32)
        s = lax.dot_general(q_sc[...], ks_ref[pl.ds(k0, tk), :], _NT, preferred_element_type=F32)
        s = biased(s, dist, (dist >= 0) & (picked > 0.5))
        _flash_step(s, vs_ref[pl.ds(k0, tk), :], m_sc, l_sc, acc_sc)
        return carry

    lax.fori_loop(0, (i * tq + tq - 1) // tk + 1, slc_block, 0)
    o_slc = acc_sc[...] * (1.0 / l_sc[...])

    wk = NSA_WINDOW + tq
    w0 = pl.multiple_of(jnp.maximum(i * tq - NSA_WINDOW, 0), tq)
    dist = rel_pos(w0, wk)
    s = lax.dot_general(q_sc[...], kw_ref[pl.ds(w0, wk), :], _NT, preferred_element_type=F32)
    s = biased(s, dist, (dist >= 0) & (dist <= NSA_WINDOW))
    e = jnp.exp(s - jnp.max(s, axis=-1, keepdims=True))
    o_win = jnp.dot(e.astype(BF16), vw_ref[pl.ds(w0, wk), :], preferred_element_type=F32)
    o_win = o_win * (1.0 / jnp.sum(e, axis=-1, keepdims=True))

    gate = 1.0 / (1.0 + jnp.exp(-z_ref[...]))
    for r in range(rep):
        rs = slice(r * tq, (r + 1) * tq)
        sl = slice(r * HEAD_DIM, (r + 1) * HEAD_DIM)
        c = r * N_NSA_BRANCHES
        o = (o_cmp[rs] * (gate[:, c:c + 1] * any_valid) + o_slc[rs] * gate[:, c + 1:c + 2]
             + o_win[rs] * gate[:, c + 2:c + 3])
        o_ref[:, sl] = _head_rmsnorm(o, g_ref[:, sl]).astype(o_ref.dtype)


def native_sparse_attention(h_b, kv_cmp, h_small, gain, batch, seq, slopes):
    m = h_b.shape[0]
    rep = len(slopes[0])
    gw = rep * HEAD_DIM
    hb = NSA_KV_GROUPS * gw
    n_cmp = seq // CMP_STRIDE
    n_sel = seq // SEL_BLOCK
    tq = 128
    tk = _pick(seq, (512, 256, 128))
    nq = seq // tq
    assert seq >= NSA_WINDOW + tq
    col = hb // HEAD_DIM

    def seq_spec(branch):
        return pl.BlockSpec((seq, HEAD_DIM), lambda b, g, i: (b, col + branch * NSA_KV_GROUPS + g))

    def cmp_spec(which):
        return pl.BlockSpec((None, n_cmp, HEAD_DIM), lambda b, g, i: (which, b * NSA_KV_GROUPS + g, 0))

    rows = rep * tq
    return pl.pallas_call(
        functools.partial(_nsa_kernel, tq=tq, tk=tk, n_cmp=n_cmp, n_sel=n_sel, slopes=slopes),
        grid=(batch, NSA_KV_GROUPS, nq),
        in_specs=[pl.BlockSpec((tq, gw), lambda b, g, i: (b * nq + i, g)),
                  cmp_spec(0), cmp_spec(1), seq_spec(2), seq_spec(3), seq_spec(4), seq_spec(5),
                  pl.BlockSpec((n_cmp, n_sel), lambda b, g, i: (0, 0)),
                  pl.BlockSpec((tq, LANES), lambda b, g, i: (b * nq + i, g)),
                  pl.BlockSpec((1, gw), lambda b, g, i: (0, g))],
        out_specs=pl.BlockSpec((tq, gw), lambda b, g, i: (b * nq + i, g)),
        out_shape=jax.ShapeDtypeStruct((m, hb), BF16),
        scratch_shapes=[pltpu.VMEM((rows, HEAD_DIM), BF16), pltpu.VMEM((tq, n_sel), BF16),
                        pltpu.VMEM((rows, LANES), F32), pltpu.VMEM((rows, LANES), F32),
                        pltpu.VMEM((rows, HEAD_DIM), F32)],
        compiler_params=_params("parallel", "parallel", "arbitrary"),
        name="native_sparse_attention",
    )(h_b, kv_cmp, kv_cmp, h_b, h_b, h_b, h_b, _overlap_matrix(seq), h_small, gain.reshape(1, hb))


def _forget_cumsum_kernel(z_ref, b_ref, c_ref, carry_sc, *, tr):
    @pl.when(pl.program_id(1) == 0)
    def _():
        carry_sc[...] = jnp.zeros(carry_sc.shape, F32)

    z = z_ref[...] + b_ref[...]
    log_f = jnp.minimum(z, 0.0) - jnp.log1p(jnp.exp(-jnp.abs(z)))
    tri = jnp.where(lax.broadcasted_iota(jnp.int32, (tr, tr), 0)
                    >= lax.broadcasted_iota(jnp.int32, (tr, tr), 1), 1.0, 0.0)
    c = jnp.dot(tri, log_f, preferred_element_type=F32, precision=lax.Precision.HIGHEST) + carry_sc[0:1, :]
    c_ref[...] = c
    carry_sc[...] = jnp.broadcast_to(c[tr - 1:tr, :], carry_sc.shape)


def forget_cumsum(h_small, tile, bias_row, batch, seq):
    m = h_small.shape[0]
    tr = _pick(seq, (256, 128))
    nb = seq // tr
    return pl.pallas_call(
        functools.partial(_forget_cumsum_kernel, tr=tr),
        grid=(batch, nb),
        in_specs=[pl.BlockSpec((tr, LANES), lambda b, i: (b * nb + i, tile)),
                  pl.BlockSpec((1, LANES), lambda b, i: (0, 0))],
        out_specs=pl.BlockSpec((tr, LANES), lambda b, i: (b * nb + i, 0)),
        out_shape=jax.ShapeDtypeStruct((m, LANES), F32),
        scratch_shapes=[pltpu.VMEM((8, LANES), F32)],
        compiler_params=_params("parallel", "arbitrary"),
        name="forget_cumsum",
    )(h_small, bias_row)


def _fox_kernel(q_ref, k_ref, v_ref, cq_ref, ck_ref, g_ref, o_ref, cq_sc, m_sc, l_sc, acc_sc,
                *, tq, tk, n_heads):
    i = pl.program_id(1)
    n_full = (i * tq) // tk
    reps = tk // LANES
    for h in range(n_heads):
        sl = slice(h * HEAD_DIM, (h + 1) * HEAD_DIM)
        cq_sc[...] = jnp.broadcast_to(cq_ref[:, h:h + 1], (tq, LANES))
        _flash_reset(m_sc, l_sc, acc_sc)

        def scores(kb):
            k0 = pl.multiple_of(kb * tk, tk)
            s = lax.dot_general(q_ref[:, sl], k_ref[pl.ds(k0, tk), sl], _NT, preferred_element_type=F32)
            return s * SCALE + pltpu.repeat(cq_sc[...], reps, axis=1) - ck_ref[h:h + 1, pl.ds(k0, tk)], k0

        def full_block(kb, carry):
            s, k0 = scores(kb)
            _flash_step(s, v_ref[pl.ds(k0, tk), sl], m_sc, l_sc, acc_sc)
            return carry

        lax.fori_loop(0, n_full, full_block, 0)
        s, k0 = scores(n_full)
        causal = (k0 + lax.broadcasted_iota(jnp.int32, (tq, tk), 1)
                  <= i * tq + lax.broadcasted_iota(jnp.int32, (tq, tk), 0))
        _flash_step(jnp.where(causal, s, NEG_INF), v_ref[pl.ds(k0, tk), sl], m_sc, l_sc, acc_sc)
        o = acc_sc[...] * (1.0 / l_sc[...])
        o_ref[:, sl] = _head_rmsnorm(o, g_ref[:, sl]).astype(o_ref.dtype)


def forgetting_attention(h_c, c_tok, c_head, gain, batch, seq):
    m, three_hc = h_c.shape
    hc = three_hc // 3
    hp = c_head.shape[0] // batch
    tq = _pick(seq, (256, 128))
    tk = _pick(seq, (512, 256, 128))
    nq = seq // tq
    resident = functools.partial(pl.BlockSpec, pipeline_mode=pl.Buffered(1))
    return pl.pallas_call(
        functools.partial(_fox_kernel, tq=tq, tk=tk, n_heads=hc // HEAD_DIM),
        grid=(batch, nq),
        in_specs=[pl.BlockSpec((tq, hc), lambda b, i: (b * nq + i, 0)),
                  resident((seq, hc), lambda b, i: (b, 1)),
                  resident((seq, hc), lambda b, i: (b, 2)),
                  pl.BlockSpec((tq, LANES), lambda b, i: (b * nq + i, 0)),
                  resident((hp, seq), lambda b, i: (b, 0)),
                  pl.BlockSpec((1, hc), lambda b, i: (0, 0))],
        out_specs=pl.BlockSpec((tq, hc), lambda b, i: (b * nq + i, 0)),
        out_shape=jax.ShapeDtypeStruct((m, hc), BF16),
        scratch_shapes=[pltpu.VMEM((tq, LANES), F32), pltpu.VMEM((tq, LANES), F32),
                        pltpu.VMEM((tq, LANES), F32), pltpu.VMEM((tq, HEAD_DIM), F32)],
        compiler_params=_params("parallel", "arbitrary"),
        name="forgetting_attention",
    )(h_c, h_c, h_c, c_tok, c_head, gain.reshape(1, hc))


def _alibi_slopes(h_dil, h_nsa):
    n = h_dil + h_nsa
    s = (2.0 ** (-8.0 * np.arange(1, n + 1) / n)).astype(np.float32)
    nsa_mask = np.isin(np.arange(n) % 5, [1, 3])
    dil = tuple(float(v) for v in s[~nsa_mask])
    nsa = s[nsa_mask].reshape(NSA_KV_GROUPS, h_nsa // NSA_KV_GROUPS)
    return dil, tuple(tuple(float(v) for v in row) for row in nsa)


def _overlap_matrix(seq):
    n_cmp = seq // CMP_STRIDE
    n_sel = seq // SEL_BLOCK
    c_start = np.arange(n_cmp) * CMP_STRIDE
    s_start = np.arange(n_sel) * SEL_BLOCK
    ov = ((c_start[:, None] < s_start[None, :] + SEL_BLOCK)
          & (c_start[:, None] + CMP_BLOCK > s_start[None, :]))
    return jnp.asarray(ov, BF16)


def _head_split(d):
    n_heads = d // HEAD_DIM
    h_dil = 3 * n_heads // 8
    h_nsa = n_heads // 4
    return h_dil, h_nsa, n_heads - h_dil - h_nsa


def _in_proj_columns(d):
    h_dil, h_nsa, h_fox = _head_split(d)
    b0 = 3 * h_dil * HEAD_DIM
    g0 = b0 + (h_nsa + 6 * NSA_KV_GROUPS) * HEAD_DIM
    c0 = g0 + h_nsa * N_NSA_BRANCHES
    f0 = c0 + 3 * h_fox * HEAD_DIM
    return b0, g0, c0, f0


def _small_projection(w_in, d):
    depth = w_in.shape[0]
    _, h_nsa, h_fox = _head_split(d)
    _, g0, _, f0 = _in_proj_columns(d)
    per_group = h_nsa // NSA_KV_GROUPS * N_NSA_BRANCHES
    tiles = []
    for g in range(NSA_KV_GROUPS):
        tiles += [w_in[:, :, g0 + g * per_group:g0 + (g + 1) * per_group],
                  jnp.zeros((depth, d, LANES - per_group), w_in.dtype)]
    tiles += [w_in[:, :, f0:f0 + h_fox], jnp.zeros((depth, d, LANES - h_fox), w_in.dtype)]
    return jnp.concatenate(tiles, axis=2)


def _layer(x, batch, seq, l, norm_mix, w_in, w_c, w_small, b_forget, cmp_pe, cmp_w1, cmp_w2, head_norm,
           w_out, norm_ffn, w_gate, w_up, w_down):
    m, d = x.shape
    h_dil, h_nsa, h_fox = _head_split(d)
    ha, hb, hc, gkv = h_dil * HEAD_DIM, h_nsa * HEAD_DIM, h_fox * HEAD_DIM, NSA_KV_GROUPS * HEAD_DIM
    slopes_dil, slopes_nsa = _alibi_slopes(h_dil, h_nsa)
    b0, g0, _, _ = _in_proj_columns(d)
    bias_row = jnp.zeros((1, LANES), F32).at[0, :h_fox].set(b_forget.astype(F32))

    xn = rmsnorm(x, norm_mix, BF16)
    h_a = matmul(xn, w_in, l, 0, b0, BF16, name="in_proj_a")
    h_b = matmul(xn, w_in, l, b0, g0 - b0, BF16, name="in_proj_b")
    h_c = matmul(xn, w_c, l, 0, 3 * hc, BF16, name="in_proj_c")
    h_small = matmul(xn, w_small, l, 0, w_small.shape[2], F32, name="in_proj_small")

    outs, lses = [], []
    for _, dil in DILATED_PATTERNS:
        o, lse = dilated_pattern(h_a, batch, seq, dil, slopes_dil)
        outs.append(o)
        lses.append(lse)
    o_a = dilated_combine(outs, lses, head_norm[:ha])

    n_chunk = seq // CMP_STRIDE

    def cmp_blocks(col):
        a = h_b[:, col:col + gkv].reshape(batch, seq, NSA_KV_GROUPS, HEAD_DIM).transpose(0, 2, 1, 3)
        chunks = a.reshape(batch, NSA_KV_GROUPS, n_chunk, CMP_STRIDE * HEAD_DIM)
        nxt = jnp.concatenate([chunks[:, :, 1:], jnp.zeros_like(chunks[:, :, :1])], axis=2)
        return jnp.concatenate([chunks, nxt], axis=-1).reshape(batch * NSA_KV_GROUPS * n_chunk, -1)

    blocks = jnp.stack([cmp_blocks(hb), cmp_blocks(hb + gkv)])
    kv_cmp = compress(blocks, cmp_pe.reshape(2, 1, CMP_BLOCK * HEAD_DIM).astype(F32),
                      cmp_w1.astype(BF16), cmp_w2.astype(BF16))
    o_b = native_sparse_attention(h_b, kv_cmp, h_small, head_norm[ha:ha + hb], batch, seq, slopes_nsa)

    c_tok = forget_cumsum(h_small, NSA_KV_GROUPS, bias_row, batch, seq)
    hp = -(-h_fox // 8) * 8
    c_head = c_tok.reshape(batch, seq, LANES)[:, :, :h_fox].transpose(0, 2, 1)
    c_head = jnp.pad(c_head, ((0, 0), (0, hp - h_fox), (0, 0))).reshape(batch * hp, seq)
    o_c = forgetting_attention(h_c, c_tok, c_head, head_norm[ha + hb:], batch, seq)

    o = jnp.concatenate([o_a, o_b, o_c], axis=1)
    x = matmul(o, w_out, l, 0, d, F32, residual=x, name="out_proj")

    hf = rmsnorm(x, norm_ffn, BF16)
    gu = swiglu(hf, w_gate, w_up, l)
    return matmul(gu, w_down, l, 0, d, F32, residual=x, name="down_proj")


def kernel(x, norm_mix, w_in, b_forget, cmp_pe_k, cmp_w1_k, cmp_w2_k, cmp_pe_v, cmp_w1_v, cmp_w2_v,
           head_norm, w_out, norm_ffn, w_gate, w_up, w_down, norm_final):
    batch, seq, d = x.shape
    depth = norm_mix.shape[0]
    _, _, c0, f0 = _in_proj_columns(d)
    w_c = w_in[:, :, c0:f0]
    w_small = _small_projection(w_in, d)
    h = x.reshape(batch * seq, d)
    for l in range(depth):
        h = _layer(h, batch, seq, l, norm_mix[l], w_in, w_c, w_small, b_forget[l],
                   jnp.stack([cmp_pe_k[l], cmp_pe_v[l]]), jnp.stack([cmp_w1_k[l], cmp_w1_v[l]]),
                   jnp.stack([cmp_w2_k[l], cmp_w2_v[l]]), head_norm[l], w_out, norm_ffn[l],
                   w_gate, w_up, w_down)
    return rmsnorm(h, norm_final, x.dtype).reshape(batch, seq, d)
```

```python
import functools

import numpy as np
import jax
import jax.numpy as jnp
from jax import lax
from jax.experimental import pallas as pl
from jax.experimental.pallas import tpu as pltpu

F32 = jnp.float32
BF16 = jnp.bfloat16

HEAD_DIM = 128
LANES = 128
NSA_KV_GROUPS = 2
DILATED_PATTERNS = ((128, 1), (512, 4), (2048, 16))
BAND = 128
CMP_BLOCK = 32
CMP_STRIDE = 16
SEL_BLOCK = 64
N_SELECT = 16
NSA_WINDOW = 512
N_NSA_BRANCHES = 3
FORCE_SCORE = 1e9
NEG_INF = -1e30
NORM_EPS = 1e-6
SCALE = HEAD_DIM ** -0.5
LOG2E = float(np.log2(np.e))
VMEM_LIMIT = 52 * 1024 * 1024
MM_VMEM_BUDGET = 44 * 1024 * 1024

_NT = (((1,), (1,)), ((), ()))


def _params(*sem):
    return pltpu.CompilerParams(dimension_semantics=sem, vmem_limit_bytes=VMEM_LIMIT)


def _pick(n, prefs):
    for t in prefs:
        if n % t == 0:
            return t
    return n


def _rmsnorm_kernel(x_ref, g_ref, o_ref):
    x = x_ref[...].astype(F32)
    ms = jnp.mean(x * x, axis=-1, keepdims=True)
    o_ref[...] = (x * lax.rsqrt(ms + NORM_EPS) * g_ref[...]).astype(o_ref.dtype)


def rmsnorm(x, g, out_dtype):
    m, d = x.shape
    tm = _pick(m, (256, 128, 8))
    return pl.pallas_call(
        _rmsnorm_kernel,
        grid=(m // tm,),
        in_specs=[pl.BlockSpec((tm, d), lambda i: (i, 0)),
                  pl.BlockSpec((1, d), lambda i: (0, 0))],
        out_specs=pl.BlockSpec((tm, d), lambda i: (i, 0)),
        out_shape=jax.ShapeDtypeStruct((m, d), out_dtype),
        compiler_params=_params("parallel"),
        name="rmsnorm",
    )(x, g.reshape(1, d).astype(F32))


def _mm_kernel(a_ref, w_ref, o_ref, wb_sc):
    @pl.when(pl.program_id(1) == 0)
    def _():
        wb_sc[...] = w_ref[...].astype(BF16)

    o_ref[...] = jnp.dot(a_ref[...], wb_sc[...], preferred_element_type=F32).astype(o_ref.dtype)


def _mm_res_kernel(a_ref, w_ref, r_ref, o_ref, wb_sc):
    @pl.when(pl.program_id(1) == 0)
    def _():
        wb_sc[...] = w_ref[...].astype(BF16)

    acc = jnp.dot(a_ref[...], wb_sc[...], preferred_element_type=F32)
    o_ref[...] = (r_ref[...] + acc).astype(o_ref.dtype)


def _swiglu_kernel(a_ref, wg_ref, wu_ref, o_ref, wg_sc, wu_sc):
    @pl.when(pl.program_id(1) == 0)
    def _():
        wg_sc[...] = wg_ref[...].astype(BF16)
        wu_sc[...] = wu_ref[...].astype(BF16)

    a = a_ref[...]
    g = jnp.dot(a, wg_sc[...], preferred_element_type=F32)
    u = jnp.dot(a, wu_sc[...], preferred_element_type=F32)
    o_ref[...] = (g * (1.0 / (1.0 + jnp.exp(-g))) * u).astype(o_ref.dtype)


def _mm_tiles(m, n, k, col0, n_weights, out_bytes):
    best = None
    for tn in (1024, 512, 256, 128):
        if n % tn or col0 % tn:
            continue
        for tm in (1024, 512, 256, 128):
            if m % tm:
                continue
            rest = 2 * tm * k * 2 + 2 * tm * tn * out_bytes + (n_weights + 1) * tm * tn * 4
            for w_bufs in (2, 1):
                need = n_weights * k * tn * (4 * w_bufs + 2) + rest
                if need <= MM_VMEM_BUDGET and (best is None or (tm * tn, w_bufs) > (best[0] * best[1], best[2])):
                    best = (tm, tn, w_bufs)
    assert best is not None, (m, n, k, col0)
    return best


def _weight_spec(k, tn, w_bufs, index_map):
    return pl.BlockSpec((None, k, tn), index_map, pipeline_mode=pl.Buffered(w_bufs))


def matmul(a, w, layer, col0, n, out_dtype, residual=None, k_block=0, k_blocks=1, name="matmul"):
    m = a.shape[0]
    k = a.shape[1] // k_blocks
    out_bytes = jnp.dtype(out_dtype).itemsize + (0 if residual is None else residual.dtype.itemsize)
    tm, tn, w_bufs = _mm_tiles(m, n, k, col0, 1, out_bytes)
    cb = col0 // tn
    in_specs = [pl.BlockSpec((tm, k), lambda j, i: (i, k_block)),
                _weight_spec(k, tn, w_bufs, lambda j, i: (layer, k_block, cb + j))]
    args = [a, w]
    kern = _mm_kernel
    if residual is not None:
        in_specs.append(pl.BlockSpec((tm, tn), lambda j, i: (i, j)))
        args.append(residual)
        kern = _mm_res_kernel
    return pl.pallas_call(
        kern,
        grid=(n // tn, m // tm),
        in_specs=in_specs,
        out_specs=pl.BlockSpec((tm, tn), lambda j, i: (i, j)),
        out_shape=jax.ShapeDtypeStruct((m, n), out_dtype),
        scratch_shapes=[pltpu.VMEM((k, tn), BF16)],
        compiler_params=_params("parallel", "arbitrary"),
        name=name,
    )(*args)


def _mm_dilated_kernel(a_ref, w_ref, *refs, dils):
    out_refs, (wb_sc, res_sc) = refs[:len(dils)], refs[len(dils):]

    @pl.when(pl.program_id(1) == 0)
    def _():
        wb_sc[...] = w_ref[...].astype(BF16)

    res = jnp.dot(a_ref[...], wb_sc[...], preferred_element_type=F32)
    n_chunks, tm, _ = res_sc.shape
    for c in range(n_chunks):
        res_sc[c] = res[:, c * LANES:(c + 1) * LANES]
    for o_ref, d in zip(out_refs, dils):
        for r in range(d):
            for c in range(n_chunks):
                o_ref[r, :, c * LANES:(c + 1) * LANES] = (
                    res_sc[c, pl.ds(r, tm // d, stride=d), :].astype(o_ref.dtype))


def matmul_dilated(a, w, layer, n, batch, seq, dils):
    m, k = a.shape
    tm, tn, w_bufs = _mm_tiles(m, n, k, 0, 1, 2 * len(dils) + 4)
    assert seq % tm == 0 and tm % (16 * max(dils)) == 0
    per_batch = seq // tm
    return pl.pallas_call(
        functools.partial(_mm_dilated_kernel, dils=dils),
        grid=(n // tn, m // tm),
        in_specs=[pl.BlockSpec((tm, k), lambda j, i: (i, 0)),
                  _weight_spec(k, tn, w_bufs, lambda j, i: (layer, 0, j))],
        out_specs=[pl.BlockSpec((None, d, tm // d, tn), lambda j, i: (i // per_batch, 0, i % per_batch, j))
                   for d in dils],
        out_shape=[jax.ShapeDtypeStruct((batch, d, seq // d, n), BF16) for d in dils],
        scratch_shapes=[pltpu.VMEM((k, tn), BF16), pltpu.VMEM((tn // LANES, tm, LANES), F32)],
        compiler_params=_params("parallel", "arbitrary"),
        name="in_proj_a",
    )(a, w)


def swiglu(a, wg, wu, layer):
    m, k = a.shape
    n = wg.shape[2]
    tm, tn, w_bufs = _mm_tiles(m, n, k, 0, 2, 2)
    w_spec = _weight_spec(k, tn, w_bufs, lambda j, i: (layer, 0, j))
    return pl.pallas_call(
        _swiglu_kernel,
        grid=(n // tn, m // tm),
        in_specs=[pl.BlockSpec((tm, k), lambda j, i: (i, 0)), w_spec, w_spec],
        out_specs=pl.BlockSpec((tm, tn), lambda j, i: (i, j)),
        out_shape=jax.ShapeDtypeStruct((m, n), BF16),
        scratch_shapes=[pltpu.VMEM((k, tn), BF16), pltpu.VMEM((k, tn), BF16)],
        compiler_params=_params("parallel", "arbitrary"),
        name="swiglu",
    )(a, wg, wu)


def _head_rmsnorm(o, g):
    ms = jnp.mean(o * o, axis=-1, keepdims=True)
    return o * lax.rsqrt(ms + NORM_EPS) * g


def _lane_tile(x, reps):
    return x if reps == 1 else jnp.concatenate([x] * reps, axis=1)


def _flash_step(s, v, m_sc, l_sc, acc_sc):
    m_prev = m_sc[...]
    m_new = jnp.maximum(m_prev, jnp.max(s, axis=-1, keepdims=True))
    alpha = jnp.exp(m_prev - m_new)
    p = jnp.exp(s - _lane_tile(m_new, s.shape[1] // LANES))
    l_sc[...] = alpha * l_sc[...] + jnp.sum(p, axis=-1, keepdims=True)
    acc_sc[...] = alpha * acc_sc[...] + jnp.dot(p.astype(BF16), v, preferred_element_type=F32)
    m_sc[...] = m_new


def _flash_reset(m_sc, l_sc, acc_sc):
    m_sc[...] = jnp.full(m_sc.shape, NEG_INF, F32)
    l_sc[...] = jnp.zeros(l_sc.shape, F32)
    acc_sc[...] = jnp.zeros(acc_sc.shape, F32)


def _dil_kernel(q_ref, kp_ref, kc_ref, vp_ref, vc_ref, o_ref, lse_ref, s_sc, p_sc, *, dil, slopes):
    i = pl.program_id(2)
    n_heads = len(slopes)
    qi = lax.broadcasted_iota(jnp.int32, (BAND, 2 * BAND), 0)
    kj = lax.broadcasted_iota(jnp.int32, (BAND, 2 * BAND), 1)
    dist = qi + BAND - kj
    first_key = jnp.where(i > 0, 0, BAND)
    ok = (dist >= 0) & (dist <= BAND) & (kj >= first_key)
    pen = (dist * dil).astype(F32)
    lane = lax.broadcasted_iota(jnp.int32, (BAND, LANES), 1)
    lse_tile = jnp.zeros((BAND, LANES), F32)
    for h in range(n_heads):
        sl = slice(h * HEAD_DIM, (h + 1) * HEAD_DIM)
        k = jnp.concatenate([kp_ref[:, sl], kc_ref[:, sl]], axis=0)
        s_sc[h] = lax.dot_general(q_ref[:, sl], k, _NT, preferred_element_type=F32)
    for h, slope in enumerate(slopes):
        s = jnp.where(ok, s_sc[h] * SCALE - slope * pen, NEG_INF)
        m = jnp.max(s, axis=-1, keepdims=True)
        e = jnp.exp(s - m)
        den = jnp.sum(e, axis=-1, keepdims=True)
        p_sc[h] = (e * (1.0 / den)).astype(BF16)
        lse_tile = jnp.where(lane == h, m + jnp.log(den), lse_tile)
    for h in range(n_heads):
        sl = slice(h * HEAD_DIM, (h + 1) * HEAD_DIM)
        v = jnp.concatenate([vp_ref[:, sl], vc_ref[:, sl]], axis=0)
        o_ref[:, sl] = jnp.dot(p_sc[h], v, preferred_element_type=F32)
    lse_ref[...] = lse_tile


def dilated_pattern(h_d, dil, slopes):
    batch, _, n_rows, three_ha = h_d.shape
    ha = three_ha // 3
    nqb = n_rows // BAND

    def spec(col, prev):
        def index_map(b, r, i):
            return (b, r, jnp.maximum(i - 1, 0) if prev else i, col)
        return pl.BlockSpec((None, None, BAND, ha), index_map)

    n_heads = len(slopes)
    return pl.pallas_call(
        functools.partial(_dil_kernel, dil=dil, slopes=slopes),
        grid=(batch, dil, nqb),
        in_specs=[spec(0, False), spec(1, True), spec(1, False), spec(2, True), spec(2, False)],
        out_specs=[spec(0, False), pl.BlockSpec((None, None, BAND, LANES), lambda b, r, i: (b, r, i, 0))],
        out_shape=[jax.ShapeDtypeStruct((batch, dil, n_rows, ha), F32),
                   jax.ShapeDtypeStruct((batch, dil, n_rows, LANES), F32)],
        scratch_shapes=[pltpu.VMEM((n_heads, BAND, 2 * BAND), F32), pltpu.VMEM((n_heads, BAND, 2 * BAND), BF16)],
        compiler_params=_params("parallel", "parallel", "arbitrary"),
        name=f"dilated_d{dil}",
    )(h_d, h_d, h_d, h_d, h_d)


def _dil_combine_kernel(*refs, n_heads, dils):
    n = len(dils)
    o_refs, l_refs = refs[:n], refs[n:2 * n]
    g_ref, out_ref = refs[2 * n], refs[2 * n + 1]
    scratch = refs[2 * n + 2:]
    tm = out_ref.shape[0]
    outs, lses = [], []
    for p, d in enumerate(dils):
        o_sc, l_sc = scratch[2 * p], scratch[2 * p + 1]
        for r in range(d):
            rows = pl.ds(r, tm // d, stride=d)
            for h in range(n_heads):
                o_sc[h, rows, :] = o_refs[p][r, :, h * HEAD_DIM:(h + 1) * HEAD_DIM]
            l_sc[rows, :] = l_refs[p][r]
        outs.append(o_sc)
        lses.append(l_sc[...])
    mx = functools.reduce(jnp.maximum, lses)
    es = [jnp.exp(l - mx) for l in lses]
    inv = 1.0 / functools.reduce(jnp.add, es)
    ws = [e * inv for e in es]
    for h in range(n_heads):
        sl = slice(h * HEAD_DIM, (h + 1) * HEAD_DIM)
        o = functools.reduce(jnp.add, [w[:, h:h + 1] * o_sc[h] for w, o_sc in zip(ws, outs)])
        out_ref[:, sl] = _head_rmsnorm(o, g_ref[:, sl]).astype(out_ref.dtype)


def dilated_combine(outs, lses, gain, dils):
    batch, _, seq, ha = outs[0].shape
    seq *= dils[0]
    tm = 256
    assert seq % tm == 0 and tm % (8 * max(dils)) == 0
    nt = seq // tm

    def spec(d, width):
        return pl.BlockSpec((None, d, tm // d, width), lambda b, i: (b, 0, i, 0))

    scratch = []
    for _ in dils:
        scratch += [pltpu.VMEM((ha // HEAD_DIM, tm, HEAD_DIM), F32), pltpu.VMEM((tm, LANES), F32)]
    return pl.pallas_call(
        functools.partial(_dil_combine_kernel, n_heads=ha // HEAD_DIM, dils=dils),
        grid=(batch, nt),
        in_specs=([spec(d, ha) for d in dils] + [spec(d, LANES) for d in dils]
                  + [pl.BlockSpec((1, ha), lambda b, i: (0, 0))]),
        out_specs=pl.BlockSpec((tm, ha), lambda b, i: (b * nt + i, 0)),
        out_shape=jax.ShapeDtypeStruct((batch * seq, ha), BF16),
        scratch_shapes=scratch,
        compiler_params=_params("parallel", "parallel"),
        name="dilated_combine",
    )(*outs, *lses, gain.reshape(1, ha))


def _compress_kernel(x_ref, pe_ref, w1_ref, w2_ref, o_ref):
    x = (x_ref[...].astype(F32) + pe_ref[...]).astype(BF16)
    hid = jnp.dot(x, w1_ref[...], preferred_element_type=F32)
    act = 0.5 * hid * (1.0 + jnp.tanh(np.sqrt(2.0 / np.pi) * (hid + 0.044715 * (hid * hid * hid))))
    o_ref[...] = jnp.dot(act.astype(BF16), w2_ref[...], preferred_element_type=F32).astype(o_ref.dtype)


def compress(blocks, pe, w1, w2):
    _, rows, width = blocks.shape
    tr = _pick(rows, (256, 128, 8))
    return pl.pallas_call(
        _compress_kernel,
        grid=(2, rows // tr),
        in_specs=[pl.BlockSpec((None, tr, width), lambda s, i: (s, i, 0)),
                  pl.BlockSpec((None, 1, width), lambda s, i: (s, 0, 0)),
                  pl.BlockSpec((None, width, HEAD_DIM), lambda s, i: (s, 0, 0)),
                  pl.BlockSpec((None, HEAD_DIM, HEAD_DIM), lambda s, i: (s, 0, 0))],
        out_specs=pl.BlockSpec((None, tr, HEAD_DIM), lambda s, i: (s, i, 0)),
        out_shape=jax.ShapeDtypeStruct((2, rows, HEAD_DIM), BF16),
        compiler_params=_params("parallel", "parallel"),
        name="nsa_compress",
    )(blocks, pe, w1, w2)


def _nsa_kernel(q_ref, kc_ref, vc_ref, ks_ref, vs_ref, kw_ref, vw_ref, ov_ref, z_ref, g_ref, o_ref,
                q_sc, sel_sc, m_sc, l_sc, acc_sc, *, tq, tk, n_cmp, n_sel, slopes):
    g = pl.program_id(1)
    i = pl.program_id(2)
    rep = len(slopes[0])
    sel_shift = int(np.log2(SEL_BLOCK))

    for r in range(rep):
        q_sc[r * tq:(r + 1) * tq, :] = q_ref[:, r * HEAD_DIM:(r + 1) * HEAD_DIM]
    head_slopes = [jnp.where(g == 0, slopes[0][r], slopes[1][r]) for r in range(rep)]

    def biased(s, dist, ok):
        dist_f = dist.astype(F32)
        return jnp.concatenate(
            [jnp.where(ok, s[r * tq:(r + 1) * tq] * SCALE - head_slopes[r] * dist_f, NEG_INF)
             for r in range(rep)], axis=0)

    def rel_pos(k0, width):
        return (lax.broadcasted_iota(jnp.int32, (tq, width), 0)
                - lax.broadcasted_iota(jnp.int32, (tq, width), 1)) + (i * tq - k0)

    t_pos = i * tq + lax.broadcasted_iota(jnp.int32, (tq, n_cmp), 0)
    c_dist = t_pos - (lax.broadcasted_iota(jnp.int32, (tq, n_cmp), 1) * CMP_STRIDE + (CMP_BLOCK - 1))
    s = lax.dot_general(q_sc[...], kc_ref[...], _NT, preferred_element_type=F32)
    s = biased(s, c_dist, c_dist >= 0)
    e = jnp.exp(s - jnp.max(s, axis=-1, keepdims=True))
    p = e * (1.0 / jnp.sum(e, axis=-1, keepdims=True))
    o_cmp = jnp.dot(p.astype(BF16), vc_ref[...], preferred_element_type=F32)
    any_valid = jnp.where(i * tq + lax.broadcasted_iota(jnp.int32, (tq, 1), 0) >= CMP_BLOCK - 1, 1.0, 0.0)

    p_sum = p[0:tq]
    for r in range(1, rep):
        p_sum = p_sum + p[r * tq:(r + 1) * tq]
    p_sum = p_sum * any_valid
    imp = lax.dot_general(ov_ref[...], p_sum.astype(BF16), _NT, preferred_element_type=F32)
    blk = lax.broadcasted_iota(jnp.int32, (n_sel, tq), 0)
    cur = lax.shift_right_logical(i * tq + lax.broadcasted_iota(jnp.int32, (n_sel, tq), 1), sel_shift)
    imp = jnp.where(blk <= cur, imp, -1.0)
    imp = jnp.where(blk == 0, FORCE_SCORE, jnp.where(blk >= cur - 1, jnp.where(blk <= cur, FORCE_SCORE, imp), imp))
    rank = jnp.zeros((n_sel, tq), jnp.int32)
    for c in range(n_sel):
        row = imp[c:c + 1, :]
        tie = jnp.where(blk > c, 1, 0)
        rank = rank + jnp.where(row > imp, 1, jnp.where(row == imp, tie, 0))
    sel_t = jnp.where(rank < min(N_SELECT, n_sel), 1.0, 0.0)
    sel_sc[...] = jnp.transpose(sel_t).astype(BF16)

    _flash_reset(m_sc, l_sc, acc_sc)

    def slc_block(kb, carry):
        k0 = pl.multiple_of(kb * tk, tk)
        dist = rel_pos(k0, tk)
        kblk = lax.shift_right_logical(k0 + lax.broadcasted_iota(jnp.int32, (n_sel, tk), 1), sel_shift)
        expand = jnp.where(lax.broadcasted_iota(jnp.int32, (n_sel, tk), 0) == kblk, 1.0, 0.0).astype(BF16)
        picked = jnp.dot(sel_sc[...], expand, preferred_element_type=F32)
        s = lax.dot_general(q_sc[...], ks_ref[pl.ds(k0, tk), :], _NT, preferred_element_type=F32)
        s = biased(s, dist, (dist >= 0) & (picked > 0.5))
        _flash_step(s, vs_ref[pl.ds(k0, tk), :], m_sc, l_sc, acc_sc)
        return carry

    lax.fori_loop(0, (i * tq + tq - 1) // tk + 1, slc_block, 0)
    o_slc = acc_sc[...] * (1.0 / l_sc[...])

    wk = NSA_WINDOW + tq
    w0 = pl.multiple_of(jnp.maximum(i * tq - NSA_WINDOW, 0), tq)
    dist = rel_pos(w0, wk)
    s = lax.dot_general(q_sc[...], kw_ref[pl.ds(w0, wk), :], _NT, preferred_element_type=F32)
    s = biased(s, dist, (dist >= 0) & (dist <= NSA_WINDOW))
    e = jnp.exp(s - jnp.max(s, axis=-1, keepdims=True))
    o_win = jnp.dot(e.astype(BF16), vw_ref[pl.ds(w0, wk), :], preferred_element_type=F32)
    o_win = o_win * (1.0 / jnp.sum(e, axis=-1, keepdims=True))

    gate = 1.0 / (1.0 + jnp.exp(-z_ref[...]))
    for r in range(rep):
        rs = slice(r * tq, (r + 1) * tq)
        sl = slice(r * HEAD_DIM, (r + 1) * HEAD_DIM)
        c = r * N_NSA_BRANCHES
        o = (o_cmp[rs] * (gate[:, c:c + 1] * any_valid) + o_slc[rs] * gate[:, c + 1:c + 2]
             + o_win[rs] * gate[:, c + 2:c + 3])
        o_ref[:, sl] = _head_rmsnorm(o, g_ref[:, sl]).astype(o_ref.dtype)


def native_sparse_attention(h_b, kv_cmp, h_small, gain, batch, seq, slopes):
    m = h_b.shape[0]
    rep = len(slopes[0])
    gw = rep * HEAD_DIM
    hb = NSA_KV_GROUPS * gw
    n_cmp = seq // CMP_STRIDE
    n_sel = seq // SEL_BLOCK
    tq = 128
    tk = _pick(seq, (512, 256, 128))
    nq = seq // tq
    assert seq >= NSA_WINDOW + tq
    col = hb // HEAD_DIM

    def seq_spec(branch):
        return pl.BlockSpec((seq, HEAD_DIM), lambda b, g, i: (b, col + branch * NSA_KV_GROUPS + g))

    def cmp_spec(which):
        return pl.BlockSpec((None, n_cmp, HEAD_DIM), lambda b, g, i: (which, b * NSA_KV_GROUPS + g, 0))

    rows = rep * tq
    return pl.pallas_call(
        functools.partial(_nsa_kernel, tq=tq, tk=tk, n_cmp=n_cmp, n_sel=n_sel, slopes=slopes),
        grid=(batch, NSA_KV_GROUPS, nq),
        in_specs=[pl.BlockSpec((tq, gw), lambda b, g, i: (b * nq + i, g)),
                  cmp_spec(0), cmp_spec(1), seq_spec(2), seq_spec(3), seq_spec(4), seq_spec(5),
                  pl.BlockSpec((n_sel, n_cmp), lambda b, g, i: (0, 0)),
                  pl.BlockSpec((tq, LANES), lambda b, g, i: (b * nq + i, g)),
                  pl.BlockSpec((1, gw), lambda b, g, i: (0, g))],
        out_specs=pl.BlockSpec((tq, gw), lambda b, g, i: (b * nq + i, g)),
        out_shape=jax.ShapeDtypeStruct((m, hb), BF16),
        scratch_shapes=[pltpu.VMEM((rows, HEAD_DIM), BF16), pltpu.VMEM((tq, n_sel), BF16),
                        pltpu.VMEM((rows, LANES), F32), pltpu.VMEM((rows, LANES), F32),
                        pltpu.VMEM((rows, HEAD_DIM), F32)],
        compiler_params=_params("parallel", "parallel", "arbitrary"),
        name="native_sparse_attention",
    )(h_b, kv_cmp, kv_cmp, h_b, h_b, h_b, h_b, _overlap_matrix(seq), h_small, gain.reshape(1, hb))


def _forget_cumsum_kernel(z_ref, b_ref, c_ref, carry_sc, *, tr):
    @pl.when(pl.program_id(1) == 0)
    def _():
        carry_sc[...] = jnp.zeros(carry_sc.shape, F32)

    z = z_ref[...] + b_ref[...]
    log_f = jnp.minimum(z, 0.0) - jnp.log1p(jnp.exp(-jnp.abs(z)))
    tri = jnp.where(lax.broadcasted_iota(jnp.int32, (tr, tr), 0)
                    >= lax.broadcasted_iota(jnp.int32, (tr, tr), 1), 1.0, 0.0)
    c = jnp.dot(tri, log_f, preferred_element_type=F32, precision=lax.Precision.HIGHEST) + carry_sc[0:1, :]
    c_ref[...] = c
    carry_sc[...] = jnp.broadcast_to(c[tr - 1:tr, :], carry_sc.shape)


def forget_cumsum(h_small, tile, bias_row, batch, seq):
    m = h_small.shape[0]
    tr = _pick(seq, (256, 128))
    nb = seq // tr
    return pl.pallas_call(
        functools.partial(_forget_cumsum_kernel, tr=tr),
        grid=(batch, nb),
        in_specs=[pl.BlockSpec((tr, LANES), lambda b, i: (b * nb + i, tile)),
                  pl.BlockSpec((1, LANES), lambda b, i: (0, 0))],
        out_specs=pl.BlockSpec((tr, LANES), lambda b, i: (b * nb + i, 0)),
        out_shape=jax.ShapeDtypeStruct((m, LANES), F32),
        scratch_shapes=[pltpu.VMEM((8, LANES), F32)],
        compiler_params=_params("parallel", "arbitrary"),
        name="forget_cumsum",
    )(h_small, bias_row)


def _fox_kernel(q_ref, k_ref, v_ref, cq_ref, ck_ref, g_ref, o_ref, cq_sc, m_sc, l_sc, acc_sc,
                *, tq, tk, n_heads, hb):
    i = pl.program_id(1)
    n_full = (i * tq) // tk
    reps = tk // LANES
    for h0 in range(0, n_heads, hb):
        heads = list(range(h0, h0 + hb))
        for b, h in enumerate(heads):
            cq_sc[b] = jnp.broadcast_to(cq_ref[:, h:h + 1] * LOG2E, (tq, LANES))
            m_sc[b] = jnp.full((tq, LANES), NEG_INF, F32)
            l_sc[b] = jnp.zeros((tq, LANES), F32)
            acc_sc[b] = jnp.zeros((tq, HEAD_DIM), F32)

        def block(kb, masked):
            k0 = pl.multiple_of(kb * tk, tk)
            rows = pl.ds(k0, tk)
            qk = [lax.dot_general(q_ref[:, h * HEAD_DIM:(h + 1) * HEAD_DIM],
                                  k_ref[rows, h * HEAD_DIM:(h + 1) * HEAD_DIM], _NT,
                                  preferred_element_type=F32) for h in heads]
            if masked:
                causal = (k0 + lax.broadcasted_iota(jnp.int32, (tq, tk), 1)
                          <= i * tq + lax.broadcasted_iota(jnp.int32, (tq, tk), 0))
            ps, alphas = [], []
            for b, h in enumerate(heads):
                s = qk[b] * (SCALE * LOG2E) + _lane_tile(cq_sc[b], reps) - ck_ref[h:h + 1, rows] * LOG2E
                if masked:
                    s = jnp.where(causal, s, NEG_INF)
                m_prev = m_sc[b]
                m_new = jnp.maximum(m_prev, jnp.max(s, axis=-1, keepdims=True))
                alpha = jnp.exp2(m_prev - m_new)
                p = jnp.exp2(s - _lane_tile(m_new, reps))
                l_sc[b] = alpha * l_sc[b] + jnp.sum(p, axis=-1, keepdims=True)
                m_sc[b] = m_new
                ps.append(p.astype(BF16))
                alphas.append(alpha)
            for b, h in enumerate(heads):
                pv = jnp.dot(ps[b], v_ref[rows, h * HEAD_DIM:(h + 1) * HEAD_DIM], preferred_element_type=F32)
                acc_sc[b] = alphas[b] * acc_sc[b] + pv

        def full_block(kb, carry):
            block(kb, False)
            return carry

        lax.fori_loop(0, n_full, full_block, 0)
        block(n_full, True)
        for b, h in enumerate(heads):
            sl = slice(h * HEAD_DIM, (h + 1) * HEAD_DIM)
            o = acc_sc[b] * (1.0 / l_sc[b])
            o_ref[:, sl] = _head_rmsnorm(o, g_ref[:, sl]).astype(o_ref.dtype)


def forgetting_attention(h_c, c_tok, c_head, gain, batch, seq):
    m, three_hc = h_c.shape
    hc = three_hc // 3
    hp = c_head.shape[0] // batch
    tq = _pick(seq, (256, 128))
    tk = _pick(seq, (512, 256, 128))
    nq = seq // tq
    n_heads = hc // HEAD_DIM
    hb = _pick(n_heads, (4, 3, 2, 1))
    resident = functools.partial(pl.BlockSpec, pipeline_mode=pl.Buffered(1))
    return pl.pallas_call(
        functools.partial(_fox_kernel, tq=tq, tk=tk, n_heads=n_heads, hb=hb),
        grid=(batch, nq),
        in_specs=[pl.BlockSpec((tq, hc), lambda b, i: (b * nq + i, 0)),
                  resident((seq, hc), lambda b, i: (b, 1)),
                  resident((seq, hc), lambda b, i: (b, 2)),
                  pl.BlockSpec((tq, LANES), lambda b, i: (b * nq + i, 0)),
                  resident((hp, seq), lambda b, i: (b, 0)),
                  pl.BlockSpec((1, hc), lambda b, i: (0, 0))],
        out_specs=pl.BlockSpec((tq, hc), lambda b, i: (b * nq + i, 0)),
        out_shape=jax.ShapeDtypeStruct((m, hc), BF16),
        scratch_shapes=[pltpu.VMEM((hb, tq, LANES), F32), pltpu.VMEM((hb, tq, LANES), F32),
                        pltpu.VMEM((hb, tq, LANES), F32), pltpu.VMEM((hb, tq, HEAD_DIM), F32)],
        compiler_params=_params("parallel", "arbitrary"),
        name="forgetting_attention",
    )(h_c, h_c, h_c, c_tok, c_head, gain.reshape(1, hc))


def _alibi_slopes(h_dil, h_nsa):
    n = h_dil + h_nsa
    s = (2.0 ** (-8.0 * np.arange(1, n + 1) / n)).astype(np.float32)
    nsa_mask = np.isin(np.arange(n) % 5, [1, 3])
    dil = tuple(float(v) for v in s[~nsa_mask])
    nsa = s[nsa_mask].reshape(NSA_KV_GROUPS, h_nsa // NSA_KV_GROUPS)
    return dil, tuple(tuple(float(v) for v in row) for row in nsa)


def _overlap_matrix(seq):
    n_cmp = seq // CMP_STRIDE
    n_sel = seq // SEL_BLOCK
    c_start = np.arange(n_cmp) * CMP_STRIDE
    s_start = np.arange(n_sel) * SEL_BLOCK
    ov = ((c_start[None, :] < s_start[:, None] + SEL_BLOCK)
          & (c_start[None, :] + CMP_BLOCK > s_start[:, None]))
    return jnp.asarray(ov, BF16)


def _head_split(d):
    n_heads = d // HEAD_DIM
    h_dil = 3 * n_heads // 8
    h_nsa = n_heads // 4
    return h_dil, h_nsa, n_heads - h_dil - h_nsa


def _in_proj_columns(d):
    h_dil, h_nsa, h_fox = _head_split(d)
    b0 = 3 * h_dil * HEAD_DIM
    g0 = b0 + (h_nsa + 6 * NSA_KV_GROUPS) * HEAD_DIM
    c0 = g0 + h_nsa * N_NSA_BRANCHES
    f0 = c0 + 3 * h_fox * HEAD_DIM
    return b0, g0, c0, f0


def _small_projection(w_in, d):
    depth = w_in.shape[0]
    _, h_nsa, h_fox = _head_split(d)
    _, g0, _, f0 = _in_proj_columns(d)
    per_group = h_nsa // NSA_KV_GROUPS * N_NSA_BRANCHES
    tiles = []
    for g in range(NSA_KV_GROUPS):
        tiles += [w_in[:, :, g0 + g * per_group:g0 + (g + 1) * per_group],
                  jnp.zeros((depth, d, LANES - per_group), w_in.dtype)]
    tiles += [w_in[:, :, f0:f0 + h_fox], jnp.zeros((depth, d, LANES - h_fox), w_in.dtype)]
    return jnp.concatenate(tiles, axis=2)


def _layer(x, batch, seq, l, norm_mix, w_in, w_c, w_small, b_forget, cmp_pe, cmp_w1, cmp_w2, head_norm,
           w_out, norm_ffn, w_gate, w_up, w_down):
    m, d = x.shape
    h_dil, h_nsa, h_fox = _head_split(d)
    ha, hb, hc, gkv = h_dil * HEAD_DIM, h_nsa * HEAD_DIM, h_fox * HEAD_DIM, NSA_KV_GROUPS * HEAD_DIM
    slopes_dil, slopes_nsa = _alibi_slopes(h_dil, h_nsa)
    b0, g0, _, _ = _in_proj_columns(d)
    bias_row = jnp.zeros((1, LANES), F32).at[0, :h_fox].set(b_forget.astype(F32))

    xn = rmsnorm(x, norm_mix, BF16)
    dils = tuple(d for _, d in DILATED_PATTERNS)
    h_a = matmul_dilated(xn, w_in, l, b0, batch, seq, dils)
    h_b = matmul(xn, w_in, l, b0, g0 - b0, BF16, name="in_proj_b")
    h_c = matmul(xn, w_c, l, 0, 3 * hc, BF16, name="in_proj_c")
    h_small = matmul(xn, w_small, l, 0, w_small.shape[2], F32, name="in_proj_small")

    outs, lses = [], []
    for h_d, dil in zip(h_a, dils):
        o, lse = dilated_pattern(h_d, dil, slopes_dil)
        outs.append(o)
        lses.append(lse)
    o_a = dilated_combine(outs, lses, head_norm[:ha], dils)

    n_chunk = seq // CMP_STRIDE

    def cmp_blocks(col):
        a = h_b[:, col:col + gkv].reshape(batch, seq, NSA_KV_GROUPS, HEAD_DIM).transpose(0, 2, 1, 3)
        chunks = a.reshape(batch, NSA_KV_GROUPS, n_chunk, CMP_STRIDE * HEAD_DIM)
        nxt = jnp.concatenate([chunks[:, :, 1:], jnp.zeros_like(chunks[:, :, :1])], axis=2)
        return jnp.concatenate([chunks, nxt], axis=-1).reshape(batch * NSA_KV_GROUPS * n_chunk, -1)

    blocks = jnp.stack([cmp_blocks(hb), cmp_blocks(hb + gkv)])
    kv_cmp = compress(blocks, cmp_pe.reshape(2, 1, CMP_BLOCK * HEAD_DIM).astype(F32),
                      cmp_w1.astype(BF16), cmp_w2.astype(BF16))
    o_b = native_sparse_attention(h_b, kv_cmp, h_small, head_norm[ha:ha + hb], batch, seq, slopes_nsa)

    c_tok = forget_cumsum(h_small, NSA_KV_GROUPS, bias_row, batch, seq)
    hp = -(-h_fox // 8) * 8
    c_head = c_tok.reshape(batch, seq, LANES)[:, :, :h_fox].transpose(0, 2, 1)
    c_head = jnp.pad(c_head, ((0, 0), (0, hp - h_fox), (0, 0))).reshape(batch * hp, seq)
    o_c = forgetting_attention(h_c, c_tok, c_head, head_norm[ha + hb:], batch, seq)

    o = jnp.concatenate([o_a, o_b, o_c], axis=1)
    x = matmul(o, w_out, l, 0, d, F32, residual=x, name="out_proj")

    hf = rmsnorm(x, norm_ffn, BF16)
    gu = swiglu(hf, w_gate, w_up, l)
    k_blocks = 2 if gu.shape[1] % (2 * LANES) == 0 else 1
    for kb in range(k_blocks):
        x = matmul(gu, w_down, l, 0, d, F32, residual=x, k_block=kb, k_blocks=k_blocks, name="down_proj")
    return x


def kernel(x, norm_mix, w_in, b_forget, cmp_pe_k, cmp_w1_k, cmp_w2_k, cmp_pe_v, cmp_w1_v, cmp_w2_v,
           head_norm, w_out, norm_ffn, w_gate, w_up, w_down, norm_final):
    batch, seq, d = x.shape
    depth = norm_mix.shape[0]
    _, _, c0, f0 = _in_proj_columns(d)
    w_c = w_in[:, :, c0:f0]
    w_small = _small_projection(w_in, d)
    h = x.reshape(batch * seq, d)
    for l in range(depth):
        h = _layer(h, batch, seq, l, norm_mix[l], w_in, w_c, w_small, b_forget[l],
                   jnp.stack([cmp_pe_k[l], cmp_pe_v[l]]), jnp.stack([cmp_w1_k[l], cmp_w1_v[l]]),
                   jnp.stack([cmp_w2_k[l], cmp_w2_v[l]]), head_norm[l], w_out, norm_ffn[l],
                   w_gate, w_up, w_down)
    return rmsnorm(h, norm_final, x.dtype).reshape(batch, seq, d)
```

```python
import functools

import numpy as np
import jax
import jax.numpy as jnp
from jax import lax
from jax.experimental import pallas as pl
from jax.experimental.pallas import tpu as pltpu

F32 = jnp.float32
BF16 = jnp.bfloat16

HEAD_DIM = 128
LANES = 128
NSA_KV_GROUPS = 2
DILATED_PATTERNS = ((128, 1), (512, 4), (2048, 16))
BAND = 128
CMP_BLOCK = 32
CMP_STRIDE = 16
SEL_BLOCK = 64
N_SELECT = 16
NSA_WINDOW = 512
N_NSA_BRANCHES = 3
FORCE_SCORE = 1e9
NEG_INF = -1e30
NORM_EPS = 1e-6
SCALE = HEAD_DIM ** -0.5
LOG2E = float(np.log2(np.e))
VMEM_LIMIT = 52 * 1024 * 1024
MM_VMEM_BUDGET = 44 * 1024 * 1024

_NT = (((1,), (1,)), ((), ()))


def _params(*sem):
    return pltpu.CompilerParams(dimension_semantics=sem, vmem_limit_bytes=VMEM_LIMIT)


def _pick(n, prefs):
    for t in prefs:
        if n % t == 0:
            return t
    return n


def _rmsnorm_kernel(x_ref, g_ref, o_ref):
    x = x_ref[...].astype(F32)
    ms = jnp.mean(x * x, axis=-1, keepdims=True)
    o_ref[...] = (x * lax.rsqrt(ms + NORM_EPS) * g_ref[...]).astype(o_ref.dtype)


def rmsnorm(x, g, out_dtype):
    m, d = x.shape
    tm = _pick(m, (256, 128, 8))
    return pl.pallas_call(
        _rmsnorm_kernel,
        grid=(m // tm,),
        in_specs=[pl.BlockSpec((tm, d), lambda i: (i, 0)),
                  pl.BlockSpec((1, d), lambda i: (0, 0))],
        out_specs=pl.BlockSpec((tm, d), lambda i: (i, 0)),
        out_shape=jax.ShapeDtypeStruct((m, d), out_dtype),
        compiler_params=_params("parallel"),
        name="rmsnorm",
    )(x, g.reshape(1, d).astype(F32))


def _stage_weight(w_ref, wb_sc, transposed):
    @pl.when(pl.program_id(1) == 0)
    def _():
        w = w_ref[...]
        wb_sc[...] = (jnp.transpose(w) if transposed else w).astype(BF16)


def _mm_kernel(a_ref, w_ref, o_ref, wb_sc, *, transposed):
    _stage_weight(w_ref, wb_sc, transposed)
    o_ref[...] = jnp.dot(a_ref[...], wb_sc[...], preferred_element_type=F32).astype(o_ref.dtype)


def _mm_res_kernel(a_ref, w_ref, r_ref, o_ref, wb_sc, *, transposed):
    _stage_weight(w_ref, wb_sc, transposed)
    acc = jnp.dot(a_ref[...], wb_sc[...], preferred_element_type=F32)
    o_ref[...] = (r_ref[...] + acc).astype(o_ref.dtype)


def _swiglu_kernel(a_ref, wg_ref, wu_ref, o_ref, wg_sc, wu_sc):
    _stage_weight(wg_ref, wg_sc, False)
    _stage_weight(wu_ref, wu_sc, False)
    a = a_ref[...]
    g = jnp.dot(a, wg_sc[...], preferred_element_type=F32)
    u = jnp.dot(a, wu_sc[...], preferred_element_type=F32)
    o_ref[...] = (g * (1.0 / (1.0 + jnp.exp(-g))) * u).astype(o_ref.dtype)


def _mm_tiles(m, n, k, col0, n_weights, out_bytes):
    best = None
    for tn in (1024, 512, 256, 128):
        if n % tn or col0 % tn:
            continue
        for tm in (1024, 512, 256, 128):
            if m % tm:
                continue
            rest = 2 * tm * k * 2 + 2 * tm * tn * out_bytes + (n_weights + 1) * tm * tn * 4
            for w_bufs in (2, 1):
                need = n_weights * k * tn * (4 * w_bufs + 2) + rest
                if need <= MM_VMEM_BUDGET and (best is None or (tm * tn, w_bufs) > (best[0] * best[1], best[2])):
                    best = (tm, tn, w_bufs)
    assert best is not None, (m, n, k, col0)
    return best


def _weight_spec(k, tn, w_bufs, layer, k_block, col_block0, transposed=False):
    if transposed:
        return pl.BlockSpec((None, tn, k), lambda j, i: (layer, col_block0 + j, k_block),
                            pipeline_mode=pl.Buffered(w_bufs))
    return pl.BlockSpec((None, k, tn), lambda j, i: (layer, k_block, col_block0 + j),
                        pipeline_mode=pl.Buffered(w_bufs))


def matmul(a, w, layer, col0, n, out_dtype, residual=None, k_block=0, k_blocks=1, transposed=False,
           name="matmul"):
    m = a.shape[0]
    k = a.shape[1] // k_blocks
    out_bytes = jnp.dtype(out_dtype).itemsize + (0 if residual is None else residual.dtype.itemsize)
    tm, tn, w_bufs = _mm_tiles(m, n, k, col0, 1, out_bytes)
    in_specs = [pl.BlockSpec((tm, k), lambda j, i: (i, k_block)),
                _weight_spec(k, tn, w_bufs, layer, k_block, col0 // tn, transposed)]
    args = [a, w]
    kern = _mm_kernel
    if residual is not None:
        in_specs.append(pl.BlockSpec((tm, tn), lambda j, i: (i, j)))
        args.append(residual)
        kern = _mm_res_kernel
    return pl.pallas_call(
        functools.partial(kern, transposed=transposed),
        grid=(n // tn, m // tm),
        in_specs=in_specs,
        out_specs=pl.BlockSpec((tm, tn), lambda j, i: (i, j)),
        out_shape=jax.ShapeDtypeStruct((m, n), out_dtype),
        scratch_shapes=[pltpu.VMEM((k, tn), BF16)],
        compiler_params=_params("parallel", "arbitrary"),
        name=name,
    )(*args)


def _mm_dilated_kernel(a_ref, w_ref, *refs, dils, transposed):
    out_refs, (wb_sc, res_sc) = refs[:len(dils)], refs[len(dils):]
    _stage_weight(w_ref, wb_sc, transposed)
    res = jnp.dot(a_ref[...], wb_sc[...], preferred_element_type=F32)
    n_chunks, tm, _ = res_sc.shape
    for c in range(n_chunks):
        res_sc[c] = res[:, c * LANES:(c + 1) * LANES]
    for o_ref, d in zip(out_refs, dils):
        for r in range(d):
            for c in range(n_chunks):
                o_ref[r, :, c * LANES:(c + 1) * LANES] = (
                    res_sc[c, pl.ds(r, tm // d, stride=d), :].astype(o_ref.dtype))


def matmul_dilated(a, w, layer, n, batch, seq, dils, transposed=False):
    m, k = a.shape
    tm, tn, w_bufs = _mm_tiles(m, n, k, 0, 1, 2 * len(dils) + 4)
    assert seq % tm == 0 and tm % (16 * max(dils)) == 0
    per_batch = seq // tm
    return pl.pallas_call(
        functools.partial(_mm_dilated_kernel, dils=dils, transposed=transposed),
        grid=(n // tn, m // tm),
        in_specs=[pl.BlockSpec((tm, k), lambda j, i: (i, 0)),
                  _weight_spec(k, tn, w_bufs, layer, 0, 0, transposed)],
        out_specs=[pl.BlockSpec((None, d, tm // d, tn), lambda j, i: (i // per_batch, 0, i % per_batch, j))
                   for d in dils],
        out_shape=[jax.ShapeDtypeStruct((batch, d, seq // d, n), BF16) for d in dils],
        scratch_shapes=[pltpu.VMEM((k, tn), BF16), pltpu.VMEM((tn // LANES, tm, LANES), F32)],
        compiler_params=_params("parallel", "arbitrary"),
        name="in_proj_a",
    )(a, w)


def swiglu(a, wg, wu, layer):
    m, k = a.shape
    n = wg.shape[2]
    tm, tn, w_bufs = _mm_tiles(m, n, k, 0, 2, 2)
    w_spec = _weight_spec(k, tn, w_bufs, layer, 0, 0)
    return pl.pallas_call(
        _swiglu_kernel,
        grid=(n // tn, m // tm),
        in_specs=[pl.BlockSpec((tm, k), lambda j, i: (i, 0)), w_spec, w_spec],
        out_specs=pl.BlockSpec((tm, tn), lambda j, i: (i, j)),
        out_shape=jax.ShapeDtypeStruct((m, n), BF16),
        scratch_shapes=[pltpu.VMEM((k, tn), BF16), pltpu.VMEM((k, tn), BF16)],
        compiler_params=_params("parallel", "arbitrary"),
        name="swiglu",
    )(a, wg, wu)


def _head_rmsnorm(o, g):
    ms = jnp.mean(o * o, axis=-1, keepdims=True)
    return o * lax.rsqrt(ms + NORM_EPS) * g


def _lane_tile(x, reps):
    return x if reps == 1 else jnp.concatenate([x] * reps, axis=1)


def _with_ones(v):
    return jnp.concatenate([v, jnp.ones((v.shape[0], LANES), v.dtype)], axis=1)


def _flash_step(s, v, m_sc, l_sc, acc_sc):
    m_prev = m_sc[...]
    m_new = jnp.maximum(m_prev, jnp.max(s, axis=-1, keepdims=True))
    alpha = jnp.exp2(m_prev - m_new)
    p = jnp.exp2(s - _lane_tile(m_new, s.shape[1] // LANES))
    l_sc[...] = alpha * l_sc[...] + jnp.sum(p, axis=-1, keepdims=True)
    acc_sc[...] = alpha * acc_sc[...] + jnp.dot(p.astype(BF16), v, preferred_element_type=F32)
    m_sc[...] = m_new


def _flash_reset(m_sc, l_sc, acc_sc):
    m_sc[...] = jnp.full(m_sc.shape, NEG_INF, F32)
    l_sc[...] = jnp.zeros(l_sc.shape, F32)
    acc_sc[...] = jnp.zeros(acc_sc.shape, F32)


def _dil_kernel(q_ref, kp_ref, kc_ref, vp_ref, vc_ref, o_ref, lse_ref, s_sc, p_sc, *, dil, slopes):
    i = pl.program_id(2)
    n_heads = len(slopes)
    qi = lax.broadcasted_iota(jnp.int32, (BAND, 2 * BAND), 0)
    kj = lax.broadcasted_iota(jnp.int32, (BAND, 2 * BAND), 1)
    dist = qi + BAND - kj
    first_key = jnp.where(i > 0, 0, BAND)
    ok = (dist >= 0) & (dist <= BAND) & (kj >= first_key)
    pen = (dist * dil).astype(F32)
    lane = lax.broadcasted_iota(jnp.int32, (BAND, LANES), 1)
    lse_tile = jnp.zeros((BAND, LANES), F32)
    for h in range(n_heads):
        sl = slice(h * HEAD_DIM, (h + 1) * HEAD_DIM)
        k = jnp.concatenate([kp_ref[:, sl], kc_ref[:, sl]], axis=0)
        s_sc[h] = lax.dot_general(q_ref[:, sl], k, _NT, preferred_element_type=F32)
    for h, slope in enumerate(slopes):
        s = jnp.where(ok, s_sc[h] * SCALE - slope * pen, NEG_INF)
        m = jnp.max(s, axis=-1, keepdims=True)
        e = jnp.exp(s - m)
        den = jnp.sum(e, axis=-1, keepdims=True)
        p_sc[h] = (e * (1.0 / den)).astype(BF16)
        lse_tile = jnp.where(lane == h, m + jnp.log(den), lse_tile)
    for h in range(n_heads):
        sl = slice(h * HEAD_DIM, (h + 1) * HEAD_DIM)
        v = jnp.concatenate([vp_ref[:, sl], vc_ref[:, sl]], axis=0)
        o_ref[:, sl] = jnp.dot(p_sc[h], v, preferred_element_type=F32).astype(o_ref.dtype)
    lse_ref[...] = lse_tile


def dilated_pattern(h_d, dil, slopes):
    batch, _, n_rows, three_ha = h_d.shape
    ha = three_ha // 3
    nqb = n_rows // BAND

    def spec(col, prev):
        def index_map(b, r, i):
            return (b, r, jnp.maximum(i - 1, 0) if prev else i, col)
        return pl.BlockSpec((None, None, BAND, ha), index_map)

    n_heads = len(slopes)
    return pl.pallas_call(
        functools.partial(_dil_kernel, dil=dil, slopes=slopes),
        grid=(batch, dil, nqb),
        in_specs=[spec(0, False), spec(1, True), spec(1, False), spec(2, True), spec(2, False)],
        out_specs=[spec(0, False), pl.BlockSpec((None, None, BAND, LANES), lambda b, r, i: (b, r, i, 0))],
        out_shape=[jax.ShapeDtypeStruct((batch, dil, n_rows, ha), BF16),
                   jax.ShapeDtypeStruct((batch, dil, n_rows, LANES), F32)],
        scratch_shapes=[pltpu.VMEM((n_heads, BAND, 2 * BAND), F32), pltpu.VMEM((n_heads, BAND, 2 * BAND), BF16)],
        compiler_params=_params("parallel", "parallel", "arbitrary"),
        name=f"dilated_d{dil}",
    )(h_d, h_d, h_d, h_d, h_d)


def _dil_combine_kernel(*refs, n_heads, dils):
    n = len(dils)
    o_refs, l_refs = refs[:n], refs[n:2 * n]
    g_ref, out_ref = refs[2 * n], refs[2 * n + 1]
    scratch = refs[2 * n + 2:]
    tm = out_ref.shape[0]
    outs, lses = [], []
    for p, d in enumerate(dils):
        o_sc, l_sc = scratch[2 * p], scratch[2 * p + 1]
        for r in range(d):
            rows = pl.ds(r, tm // d, stride=d)
            for h in range(n_heads):
                o_sc[h, rows, :] = o_refs[p][r, :, h * HEAD_DIM:(h + 1) * HEAD_DIM].astype(F32)
            l_sc[rows, :] = l_refs[p][r]
        outs.append(o_sc)
        lses.append(l_sc[...])
    mx = functools.reduce(jnp.maximum, lses)
    es = [jnp.exp(l - mx) for l in lses]
    inv = 1.0 / functools.reduce(jnp.add, es)
    ws = [e * inv for e in es]
    for h in range(n_heads):
        sl = slice(h * HEAD_DIM, (h + 1) * HEAD_DIM)
        o = functools.reduce(jnp.add, [w[:, h:h + 1] * o_sc[h] for w, o_sc in zip(ws, outs)])
        out_ref[:, sl] = _head_rmsnorm(o, g_ref[:, sl]).astype(out_ref.dtype)


def dilated_combine(outs, lses, gain, dils):
    batch, _, seq, ha = outs[0].shape
    seq *= dils[0]
    tm = 256
    assert seq % tm == 0 and tm % (8 * max(dils)) == 0
    nt = seq // tm

    def spec(d, width):
        return pl.BlockSpec((None, d, tm // d, width), lambda b, i: (b, 0, i, 0))

    scratch = []
    for _ in dils:
        scratch += [pltpu.VMEM((ha // HEAD_DIM, tm, HEAD_DIM), F32), pltpu.VMEM((tm, LANES), F32)]
    return pl.pallas_call(
        functools.partial(_dil_combine_kernel, n_heads=ha // HEAD_DIM, dils=dils),
        grid=(batch, nt),
        in_specs=([spec(d, ha) for d in dils] + [spec(d, LANES) for d in dils]
                  + [pl.BlockSpec((1, ha), lambda b, i: (0, 0))]),
        out_specs=pl.BlockSpec((tm, ha), lambda b, i: (b * nt + i, 0)),
        out_shape=jax.ShapeDtypeStruct((batch * seq, ha), BF16),
        scratch_shapes=scratch,
        compiler_params=_params("parallel", "parallel"),
        name="dilated_combine",
    )(*outs, *lses, gain.reshape(1, ha))


def _compress_kernel(x_ref, pe_ref, w1_ref, w2_ref, o_ref):
    x = (x_ref[...].astype(F32) + pe_ref[...]).astype(BF16)
    hid = jnp.dot(x, w1_ref[...], preferred_element_type=F32)
    act = 0.5 * hid * (1.0 + jnp.tanh(np.sqrt(2.0 / np.pi) * (hid + 0.044715 * (hid * hid * hid))))
    o_ref[...] = jnp.dot(act.astype(BF16), w2_ref[...], preferred_element_type=F32).astype(o_ref.dtype)


def compress(blocks, pe, w1, w2):
    _, rows, width = blocks.shape
    tr = _pick(rows, (256, 128, 8))
    return pl.pallas_call(
        _compress_kernel,
        grid=(2, rows // tr),
        in_specs=[pl.BlockSpec((None, tr, width), lambda s, i: (s, i, 0)),
                  pl.BlockSpec((None, 1, width), lambda s, i: (s, 0, 0)),
                  pl.BlockSpec((None, width, HEAD_DIM), lambda s, i: (s, 0, 0)),
                  pl.BlockSpec((None, HEAD_DIM, HEAD_DIM), lambda s, i: (s, 0, 0))],
        out_specs=pl.BlockSpec((None, tr, HEAD_DIM), lambda s, i: (s, i, 0)),
        out_shape=jax.ShapeDtypeStruct((2, rows, HEAD_DIM), BF16),
        compiler_params=_params("parallel", "parallel"),
        name="nsa_compress",
    )(blocks, pe, w1, w2)


def _nsa_kernel(q_ref, kc_ref, vc_ref, ks_ref, vs_ref, kw_ref, vw_ref, ov_ref, z_ref, g_ref, o_ref,
                q_sc, sel_sc, m_sc, l_sc, acc_sc, *, tq, tk, n_cmp, n_sel, slopes):
    g = pl.program_id(1)
    i = pl.program_id(2)
    rep = len(slopes[0])
    sel_shift = int(np.log2(SEL_BLOCK))

    for r in range(rep):
        q_sc[r * tq:(r + 1) * tq, :] = q_ref[:, r * HEAD_DIM:(r + 1) * HEAD_DIM]
    head_slopes = [jnp.where(g == 0, slopes[0][r] * LOG2E, slopes[1][r] * LOG2E) for r in range(rep)]

    def biased(s, dist, ok):
        dist_f = dist.astype(F32)
        return jnp.concatenate(
            [jnp.where(ok, s[r * tq:(r + 1) * tq] * (SCALE * LOG2E) - head_slopes[r] * dist_f, NEG_INF)
             for r in range(rep)], axis=0)

    def rel_pos(k0, width):
        return (lax.broadcasted_iota(jnp.int32, (tq, width), 0)
                - lax.broadcasted_iota(jnp.int32, (tq, width), 1)) + (i * tq - k0)

    t_pos = i * tq + lax.broadcasted_iota(jnp.int32, (tq, n_cmp), 0)
    c_dist = t_pos - (lax.broadcasted_iota(jnp.int32, (tq, n_cmp), 1) * CMP_STRIDE + (CMP_BLOCK - 1))
    s = lax.dot_general(q_sc[...], kc_ref[...], _NT, preferred_element_type=F32)
    s = biased(s, c_dist, c_dist >= 0)
    e = jnp.exp2(s - jnp.max(s, axis=-1, keepdims=True))
    p = e * (1.0 / jnp.sum(e, axis=-1, keepdims=True))
    o_cmp = jnp.dot(p.astype(BF16), vc_ref[...], preferred_element_type=F32)
    any_valid = jnp.where(i * tq + lax.broadcasted_iota(jnp.int32, (tq, 1), 0) >= CMP_BLOCK - 1, 1.0, 0.0)

    p_sum = p[0:tq]
    for r in range(1, rep):
        p_sum = p_sum + p[r * tq:(r + 1) * tq]
    p_sum = p_sum * any_valid
    imp = lax.dot_general(ov_ref[...], p_sum.astype(BF16), _NT, preferred_element_type=F32)
    blk = lax.broadcasted_iota(jnp.int32, (n_sel, tq), 0)
    cur = lax.shift_right_logical(i * tq + lax.broadcasted_iota(jnp.int32, (n_sel, tq), 1), sel_shift)
    imp = jnp.where(blk <= cur, imp, -1.0)
    imp = jnp.where(blk == 0, FORCE_SCORE, jnp.where(blk >= cur - 1, jnp.where(blk <= cur, FORCE_SCORE, imp), imp))
    rank = jnp.zeros((n_sel, tq), jnp.int32)
    for c in range(n_sel):
        row = imp[c:c + 1, :]
        tie = jnp.where(blk > c, 1, 0)
        rank = rank + jnp.where(row > imp, 1, jnp.where(row == imp, tie, 0))
    sel_t = jnp.where(rank < min(N_SELECT, n_sel), 1.0, 0.0)
    sel_sc[...] = jnp.transpose(sel_t).astype(BF16)

    _flash_reset(m_sc, l_sc, acc_sc)

    def slc_block(kb, carry):
        k0 = pl.multiple_of(kb * tk, tk)
        dist = rel_pos(k0, tk)
        kblk = lax.shift_right_logical(k0 + lax.broadcasted_iota(jnp.int32, (n_sel, tk), 1), sel_shift)
        expand = jnp.where(lax.broadcasted_iota(jnp.int32, (n_sel, tk), 0) == kblk, 1.0, 0.0).astype(BF16)
        picked = jnp.dot(sel_sc[...], expand, preferred_element_type=F32)
        s = lax.dot_general(q_sc[...], ks_ref[pl.ds(k0, tk), :], _NT, preferred_element_type=F32)
        s = biased(s, dist, (dist >= 0) & (picked > 0.5))
        _flash_step(s, vs_ref[pl.ds(k0, tk), :], m_sc, l_sc, acc_sc)
        return carry

    lax.fori_loop(0, (i * tq + tq - 1) // tk + 1, slc_block, 0)
    o_slc = acc_sc[...] * (1.0 / l_sc[...])

    wk = NSA_WINDOW + tq
    w0 = pl.multiple_of(jnp.maximum(i * tq - NSA_WINDOW, 0), tq)
    dist = rel_pos(w0, wk)
    s = lax.dot_general(q_sc[...], kw_ref[pl.ds(w0, wk), :], _NT, preferred_element_type=F32)
    s = biased(s, dist, (dist >= 0) & (dist <= NSA_WINDOW))
    e = jnp.exp2(s - jnp.max(s, axis=-1, keepdims=True))
    o_win = jnp.dot(e.astype(BF16), vw_ref[pl.ds(w0, wk), :], preferred_element_type=F32)
    o_win = o_win * (1.0 / jnp.sum(e, axis=-1, keepdims=True))

    gate = 1.0 / (1.0 + jnp.exp(-z_ref[...]))
    for r in range(rep):
        rs = slice(r * tq, (r + 1) * tq)
        sl = slice(r * HEAD_DIM, (r + 1) * HEAD_DIM)
        c = r * N_NSA_BRANCHES
        o = (o_cmp[rs] * (gate[:, c:c + 1] * any_valid) + o_slc[rs] * gate[:, c + 1:c + 2]
             + o_win[rs] * gate[:, c + 2:c + 3])
        o_ref[:, sl] = _head_rmsnorm(o, g_ref[:, sl]).astype(o_ref.dtype)


def native_sparse_attention(h_b, kv_cmp, h_small, gain, batch, seq, slopes):
    m = h_b.shape[0]
    rep = len(slopes[0])
    gw = rep * HEAD_DIM
    hb = NSA_KV_GROUPS * gw
    n_cmp = seq // CMP_STRIDE
    n_sel = seq // SEL_BLOCK
    tq = 256
    tk = _pick(seq, (512, 256, 128))
    nq = seq // tq
    assert seq >= NSA_WINDOW + tq
    col = hb // HEAD_DIM

    def seq_spec(branch):
        return pl.BlockSpec((seq, HEAD_DIM), lambda b, g, i: (b, col + branch * NSA_KV_GROUPS + g))

    def cmp_spec(which):
        return pl.BlockSpec((None, n_cmp, HEAD_DIM), lambda b, g, i: (which, b * NSA_KV_GROUPS + g, 0))

    rows = rep * tq
    return pl.pallas_call(
        functools.partial(_nsa_kernel, tq=tq, tk=tk, n_cmp=n_cmp, n_sel=n_sel, slopes=slopes),
        grid=(batch, NSA_KV_GROUPS, nq),
        in_specs=[pl.BlockSpec((tq, gw), lambda b, g, i: (b * nq + i, g)),
                  cmp_spec(0), cmp_spec(1), seq_spec(2), seq_spec(3), seq_spec(4), seq_spec(5),
                  pl.BlockSpec((n_sel, n_cmp), lambda b, g, i: (0, 0)),
                  pl.BlockSpec((tq, LANES), lambda b, g, i: (b * nq + i, g)),
                  pl.BlockSpec((1, gw), lambda b, g, i: (0, g))],
        out_specs=pl.BlockSpec((tq, gw), lambda b, g, i: (b * nq + i, g)),
        out_shape=jax.ShapeDtypeStruct((m, hb), BF16),
        scratch_shapes=[pltpu.VMEM((rows, HEAD_DIM), BF16), pltpu.VMEM((tq, n_sel), BF16),
                        pltpu.VMEM((rows, LANES), F32), pltpu.VMEM((rows, LANES), F32),
                        pltpu.VMEM((rows, HEAD_DIM), F32)],
        compiler_params=_params("parallel", "parallel", "arbitrary"),
        name="native_sparse_attention",
    )(h_b, kv_cmp, kv_cmp, h_b, h_b, h_b, h_b, _overlap_matrix(seq), h_small, gain.reshape(1, hb))


def _forget_cumsum_kernel(z_ref, b_ref, c_ref, carry_sc, *, tr):
    @pl.when(pl.program_id(1) == 0)
    def _():
        carry_sc[...] = jnp.zeros(carry_sc.shape, F32)

    z = z_ref[...] + b_ref[...]
    log_f = jnp.minimum(z, 0.0) - jnp.log1p(jnp.exp(-jnp.abs(z)))
    tri = jnp.where(lax.broadcasted_iota(jnp.int32, (tr, tr), 0)
                    >= lax.broadcasted_iota(jnp.int32, (tr, tr), 1), 1.0, 0.0)
    c = jnp.dot(tri, log_f, preferred_element_type=F32, precision=lax.Precision.HIGHEST) + carry_sc[0:1, :]
    c_ref[...] = c
    carry_sc[...] = jnp.broadcast_to(c[tr - 1:tr, :], carry_sc.shape)


def forget_cumsum(h_small, tile, bias_row, batch, seq):
    m = h_small.shape[0]
    tr = _pick(seq, (256, 128))
    nb = seq // tr
    return pl.pallas_call(
        functools.partial(_forget_cumsum_kernel, tr=tr),
        grid=(batch, nb),
        in_specs=[pl.BlockSpec((tr, LANES), lambda b, i: (b * nb + i, tile)),
                  pl.BlockSpec((1, LANES), lambda b, i: (0, 0))],
        out_specs=pl.BlockSpec((tr, LANES), lambda b, i: (b * nb + i, 0)),
        out_shape=jax.ShapeDtypeStruct((m, LANES), F32),
        scratch_shapes=[pltpu.VMEM((8, LANES), F32)],
        compiler_params=_params("parallel", "arbitrary"),
        name="forget_cumsum",
    )(h_small, bias_row)


def _fox_kernel(q_ref, k_ref, v_ref, cq_ref, ck_ref, g_ref, o_ref, cq_sc, m_sc, l_sc, acc_sc,
                *, tq, tk, n_heads, hb):
    i = pl.program_id(1)
    n_full = (i * tq) // tk
    reps = tk // LANES
    for h0 in range(0, n_heads, hb):
        heads = list(range(h0, h0 + hb))
        for b, h in enumerate(heads):
            cq_sc[b] = jnp.broadcast_to(cq_ref[:, h:h + 1] * LOG2E, (tq, LANES))
            m_sc[b] = jnp.full((tq, LANES), NEG_INF, F32)
            l_sc[b] = jnp.zeros((tq, LANES), F32)
            acc_sc[b] = jnp.zeros((tq, HEAD_DIM), F32)

        def block(kb, masked):
            k0 = pl.multiple_of(kb * tk, tk)
            rows = pl.ds(k0, tk)
            qk = [lax.dot_general(q_ref[:, h * HEAD_DIM:(h + 1) * HEAD_DIM],
                                  k_ref[rows, h * HEAD_DIM:(h + 1) * HEAD_DIM], _NT,
                                  preferred_element_type=F32) for h in heads]
            if masked:
                causal = (k0 + lax.broadcasted_iota(jnp.int32, (tq, tk), 1)
                          <= i * tq + lax.broadcasted_iota(jnp.int32, (tq, tk), 0))
            ps, alphas = [], []
            for b, h in enumerate(heads):
                s = qk[b] * (SCALE * LOG2E) + _lane_tile(cq_sc[b], reps) - ck_ref[h:h + 1, rows] * LOG2E
                if masked:
                    s = jnp.where(causal, s, NEG_INF)
                m_prev = m_sc[b]
                m_new = jnp.maximum(m_prev, jnp.max(s, axis=-1, keepdims=True))
                alpha = jnp.exp2(m_prev - m_new)
                ps.append(jnp.exp2(s - _lane_tile(m_new, reps)).astype(BF16))
                m_sc[b] = m_new
                alphas.append(alpha)
            for b, h in enumerate(heads):
                pv = jnp.dot(ps[b], _with_ones(v_ref[rows, h * HEAD_DIM:(h + 1) * HEAD_DIM]),
                             preferred_element_type=F32)
                acc_sc[b] = alphas[b] * acc_sc[b] + pv[:, :HEAD_DIM]
                l_sc[b] = alphas[b] * l_sc[b] + pv[:, HEAD_DIM:]

        def full_block(kb, carry):
            block(kb, False)
            return carry

        lax.fori_loop(0, n_full, full_block, 0)
        block(n_full, True)
        for b, h in enumerate(heads):
            sl = slice(h * HEAD_DIM, (h + 1) * HEAD_DIM)
            o = acc_sc[b] * (1.0 / l_sc[b])
            o_ref[:, sl] = _head_rmsnorm(o, g_ref[:, sl]).astype(o_ref.dtype)


def forgetting_attention(h_c, c_tok, c_head, gain, batch, seq):
    m, three_hc = h_c.shape
    hc = three_hc // 3
    hp = c_head.shape[0] // batch
    tq = _pick(seq, (256, 128))
    tk = _pick(seq, (512, 256, 128))
    nq = seq // tq
    n_heads = hc // HEAD_DIM
    hb = _pick(n_heads, (4, 3, 2, 1))
    resident = functools.partial(pl.BlockSpec, pipeline_mode=pl.Buffered(1))
    return pl.pallas_call(
        functools.partial(_fox_kernel, tq=tq, tk=tk, n_heads=n_heads, hb=hb),
        grid=(batch, nq),
        in_specs=[pl.BlockSpec((tq, hc), lambda b, i: (b * nq + i, 0)),
                  resident((seq, hc), lambda b, i: (b, 1)),
                  resident((seq, hc), lambda b, i: (b, 2)),
                  pl.BlockSpec((tq, LANES), lambda b, i: (b * nq + i, 0)),
                  resident((hp, seq), lambda b, i: (b, 0)),
                  pl.BlockSpec((1, hc), lambda b, i: (0, 0))],
        out_specs=pl.BlockSpec((tq, hc), lambda b, i: (b * nq + i, 0)),
        out_shape=jax.ShapeDtypeStruct((m, hc), BF16),
        scratch_shapes=[pltpu.VMEM((hb, tq, LANES), F32), pltpu.VMEM((hb, tq, LANES), F32),
                        pltpu.VMEM((hb, tq, LANES), F32), pltpu.VMEM((hb, tq, HEAD_DIM), F32)],
        compiler_params=_params("parallel", "arbitrary"),
        name="forgetting_attention",
    )(h_c, h_c, h_c, c_tok, c_head, gain.reshape(1, hc))


def _alibi_slopes(h_dil, h_nsa):
    n = h_dil + h_nsa
    s = (2.0 ** (-8.0 * np.arange(1, n + 1) / n)).astype(np.float32)
    nsa_mask = np.isin(np.arange(n) % 5, [1, 3])
    dil = tuple(float(v) for v in s[~nsa_mask])
    nsa = s[nsa_mask].reshape(NSA_KV_GROUPS, h_nsa // NSA_KV_GROUPS)
    return dil, tuple(tuple(float(v) for v in row) for row in nsa)


def _overlap_matrix(seq):
    n_cmp = seq // CMP_STRIDE
    n_sel = seq // SEL_BLOCK
    c_start = np.arange(n_cmp) * CMP_STRIDE
    s_start = np.arange(n_sel) * SEL_BLOCK
    ov = ((c_start[None, :] < s_start[:, None] + SEL_BLOCK)
          & (c_start[None, :] + CMP_BLOCK > s_start[:, None]))
    return jnp.asarray(ov, BF16)


def _head_split(d):
    n_heads = d // HEAD_DIM
    h_dil = 3 * n_heads // 8
    h_nsa = n_heads // 4
    return h_dil, h_nsa, n_heads - h_dil - h_nsa


def _in_proj_columns(d):
    h_dil, h_nsa, h_fox = _head_split(d)
    b0 = 3 * h_dil * HEAD_DIM
    g0 = b0 + (h_nsa + 6 * NSA_KV_GROUPS) * HEAD_DIM
    c0 = g0 + h_nsa * N_NSA_BRANCHES
    f0 = c0 + 3 * h_fox * HEAD_DIM
    return b0, g0, c0, f0


def _small_projection(w_in, d):
    depth = w_in.shape[0]
    _, h_nsa, h_fox = _head_split(d)
    _, g0, _, f0 = _in_proj_columns(d)
    per_group = h_nsa // NSA_KV_GROUPS * N_NSA_BRANCHES
    tiles = []
    for g in range(NSA_KV_GROUPS):
        tiles += [w_in[:, :, g0 + g * per_group:g0 + (g + 1) * per_group],
                  jnp.zeros((depth, d, LANES - per_group), w_in.dtype)]
    tiles += [w_in[:, :, f0:f0 + h_fox], jnp.zeros((depth, d, LANES - h_fox), w_in.dtype)]
    return jnp.concatenate(tiles, axis=2)


def _layer(x, batch, seq, l, norm_mix, w_ab, w_c, w_small, b_forget, cmp_pe, cmp_w1, cmp_w2, head_norm,
           w_out, norm_ffn, w_gate, w_up, w_down):
    m, d = x.shape
    h_dil, h_nsa, h_fox = _head_split(d)
    ha, hb, hc, gkv = h_dil * HEAD_DIM, h_nsa * HEAD_DIM, h_fox * HEAD_DIM, NSA_KV_GROUPS * HEAD_DIM
    slopes_dil, slopes_nsa = _alibi_slopes(h_dil, h_nsa)
    b0, g0, _, _ = _in_proj_columns(d)
    bias_row = jnp.zeros((1, LANES), F32).at[0, :h_fox].set(b_forget.astype(F32))

    xn = rmsnorm(x, norm_mix, BF16)
    dils = tuple(d for _, d in DILATED_PATTERNS)
    h_a = matmul_dilated(xn, w_ab, l, b0, batch, seq, dils, transposed=True)
    h_b = matmul(xn, w_ab, l, b0, g0 - b0, BF16, transposed=True, name="in_proj_b")
    h_c = matmul(xn, w_c, l, 0, 3 * hc, BF16, transposed=True, name="in_proj_c")
    h_small = matmul(xn, w_small, l, 0, w_small.shape[2], F32, name="in_proj_small")

    outs, lses = [], []
    for h_d, dil in zip(h_a, dils):
        o, lse = dilated_pattern(h_d, dil, slopes_dil)
        outs.append(o)
        lses.append(lse)
    o_a = dilated_combine(outs, lses, head_norm[:ha], dils)

    n_chunk = seq // CMP_STRIDE

    def cmp_blocks(col):
        a = h_b[:, col:col + gkv].reshape(batch, seq, NSA_KV_GROUPS, HEAD_DIM).transpose(0, 2, 1, 3)
        chunks = a.reshape(batch, NSA_KV_GROUPS, n_chunk, CMP_STRIDE * HEAD_DIM)
        nxt = jnp.concatenate([chunks[:, :, 1:], jnp.zeros_like(chunks[:, :, :1])], axis=2)
        return jnp.concatenate([chunks, nxt], axis=-1).reshape(batch * NSA_KV_GROUPS * n_chunk, -1)

    blocks = jnp.stack([cmp_blocks(hb), cmp_blocks(hb + gkv)])
    kv_cmp = compress(blocks, cmp_pe.reshape(2, 1, CMP_BLOCK * HEAD_DIM).astype(F32),
                      cmp_w1.astype(BF16), cmp_w2.astype(BF16))
    o_b = native_sparse_attention(h_b, kv_cmp, h_small, head_norm[ha:ha + hb], batch, seq, slopes_nsa)

    c_tok = forget_cumsum(h_small, NSA_KV_GROUPS, bias_row, batch, seq)
    hp = -(-h_fox // 8) * 8
    c_head = c_tok.reshape(batch, seq, LANES)[:, :, :h_fox].transpose(0, 2, 1)
    c_head = jnp.pad(c_head, ((0, 0), (0, hp - h_fox), (0, 0))).reshape(batch * hp, seq)
    o_c = forgetting_attention(h_c, c_tok, c_head, head_norm[ha + hb:], batch, seq)

    o = jnp.concatenate([o_a, o_b, o_c], axis=1)
    x = matmul(o, w_out, l, 0, d, F32, residual=x, name="out_proj")

    hf = rmsnorm(x, norm_ffn, BF16)
    gu = swiglu(hf, w_gate, w_up, l)
    k_blocks = 2 if gu.shape[1] % (2 * LANES) == 0 else 1
    for kb in range(k_blocks):
        x = matmul(gu, w_down, l, 0, d, F32, residual=x, k_block=kb, k_blocks=k_blocks, name="down_proj")
    return x


def kernel(x, norm_mix, w_in, b_forget, cmp_pe_k, cmp_w1_k, cmp_w2_k, cmp_pe_v, cmp_w1_v, cmp_w2_v,
           head_norm, w_out, norm_ffn, w_gate, w_up, w_down, norm_final):
    batch, seq, d = x.shape
    depth = norm_mix.shape[0]
    _, g0, c0, f0 = _in_proj_columns(d)
    w_ab = jnp.transpose(w_in[:, :, :g0], (0, 2, 1))
    w_c = jnp.transpose(w_in[:, :, c0:f0], (0, 2, 1))
    w_small = _small_projection(w_in, d)
    h = x.reshape(batch * seq, d)
    for l in range(depth):
        h = _layer(h, batch, seq, l, norm_mix[l], w_ab, w_c, w_small, b_forget[l],
                   jnp.stack([cmp_pe_k[l], cmp_pe_v[l]]), jnp.stack([cmp_w1_k[l], cmp_w1_v[l]]),
                   jnp.stack([cmp_w2_k[l], cmp_w2_v[l]]), head_norm[l], w_out, norm_ffn[l],
                   w_gate, w_up, w_down)
    return rmsnorm(h, norm_final, x.dtype).reshape(batch, seq, d)
```

```python
import functools

import numpy as np
import jax
import jax.numpy as jnp
from jax import lax
from jax.experimental import pallas as pl
from jax.experimental.pallas import tpu as pltpu

F32 = jnp.float32
BF16 = jnp.bfloat16

HEAD_DIM = 128
LANES = 128
NSA_KV_GROUPS = 2
DILATED_PATTERNS = ((128, 1), (512, 4), (2048, 16))
BAND = 128
CMP_BLOCK = 32
CMP_STRIDE = 16
SEL_BLOCK = 64
N_SELECT = 16
NSA_WINDOW = 512
N_NSA_BRANCHES = 3
FORCE_SCORE = 1e9
NEG_INF = -1e30
NORM_EPS = 1e-6
SCALE = HEAD_DIM ** -0.5
LOG2E = float(np.log2(np.e))
VMEM_LIMIT = 58 * 1024 * 1024
MM_VMEM_BUDGET = 51 * 1024 * 1024

_NT = (((1,), (1,)), ((), ()))


def _params(*sem):
    return pltpu.CompilerParams(dimension_semantics=sem, vmem_limit_bytes=VMEM_LIMIT)


def _pick(n, prefs):
    for t in prefs:
        if n % t == 0:
            return t
    return n


def _rmsnorm_kernel(x_ref, g_ref, o_ref):
    x = x_ref[...].astype(F32)
    ms = jnp.mean(x * x, axis=-1, keepdims=True)
    o_ref[...] = (x * lax.rsqrt(ms + NORM_EPS) * g_ref[...]).astype(o_ref.dtype)


def rmsnorm(x, g, out_dtype):
    m, d = x.shape
    tm = _pick(m, (256, 128, 8))
    return pl.pallas_call(
        _rmsnorm_kernel,
        grid=(m // tm,),
        in_specs=[pl.BlockSpec((tm, d), lambda i: (i, 0)),
                  pl.BlockSpec((1, d), lambda i: (0, 0))],
        out_specs=pl.BlockSpec((tm, d), lambda i: (i, 0)),
        out_shape=jax.ShapeDtypeStruct((m, d), out_dtype),
        compiler_params=_params("parallel"),
        name="rmsnorm",
    )(x, g.reshape(1, d).astype(F32))


def _staged(w_ref, wb_sc):
    if wb_sc is None:
        return w_ref

    @pl.when(pl.program_id(1) == 0)
    def _():
        wb_sc[...] = w_ref[...].astype(BF16)

    return wb_sc


def _mm_kernel(a_ref, w_ref, o_ref, wb_sc=None):
    w = _staged(w_ref, wb_sc)
    o_ref[...] = jnp.dot(a_ref[...], w[...], preferred_element_type=F32).astype(o_ref.dtype)


def _mm_res_kernel(a_ref, w_ref, r_ref, o_ref, wb_sc=None):
    w = _staged(w_ref, wb_sc)
    acc = jnp.dot(a_ref[...], w[...], preferred_element_type=F32)
    o_ref[...] = (r_ref[...] + acc).astype(o_ref.dtype)


def _swiglu_kernel(a_ref, wg_ref, wu_ref, o_ref, wg_sc, wu_sc):
    wg = _staged(wg_ref, wg_sc)
    wu = _staged(wu_ref, wu_sc)
    a = a_ref[...]
    g = jnp.dot(a, wg[...], preferred_element_type=F32)
    u = jnp.dot(a, wu[...], preferred_element_type=F32)
    o_ref[...] = (g * (1.0 / (1.0 + jnp.exp(-g))) * u).astype(o_ref.dtype)


def _mm_tiles(m, n, k, col0, n_weights, out_bytes, w_bytes):
    best = None
    staged = 2 if w_bytes == 4 else 0
    for tn in (1024, 512, 256, 128):
        if n % tn or col0 % tn:
            continue
        for tm in (1024, 512, 256, 128):
            if m % tm:
                continue
            rest = 2 * tm * k * 2 + 2 * tm * tn * out_bytes + (n_weights + 1) * tm * tn * 4
            for w_bufs in (2, 1):
                need = n_weights * k * tn * (w_bytes * w_bufs + staged) + rest
                if need <= MM_VMEM_BUDGET and (best is None or (tm * tn, w_bufs) > (best[0] * best[1], best[2])):
                    best = (tm, tn, w_bufs)
    assert best is not None, (m, n, k, col0)
    return best


def _weight_spec(k, tn, w_bufs, layer, k_block, col_block0):
    return pl.BlockSpec((None, k, tn), lambda j, i: (layer, k_block, col_block0 + j),
                        pipeline_mode=pl.Buffered(w_bufs))


def _staging(w, k, tn, n_weights=1):
    return [pltpu.VMEM((k, tn), BF16)] * n_weights if w.dtype == F32 else []


def matmul(a, w, layer, col0, n, out_dtype, residual=None, k_block=0, k_blocks=1, name="matmul"):
    m = a.shape[0]
    k = a.shape[1] // k_blocks
    out_bytes = jnp.dtype(out_dtype).itemsize + (0 if residual is None else residual.dtype.itemsize)
    tm, tn, w_bufs = _mm_tiles(m, n, k, col0, 1, out_bytes, w.dtype.itemsize)
    in_specs = [pl.BlockSpec((tm, k), lambda j, i: (i, k_block)),
                _weight_spec(k, tn, w_bufs, layer, k_block, col0 // tn)]
    args = [a, w]
    kern = _mm_kernel
    if residual is not None:
        in_specs.append(pl.BlockSpec((tm, tn), lambda j, i: (i, j)))
        args.append(residual)
        kern = _mm_res_kernel
    return pl.pallas_call(
        kern,
        grid=(n // tn, m // tm),
        in_specs=in_specs,
        out_specs=pl.BlockSpec((tm, tn), lambda j, i: (i, j)),
        out_shape=jax.ShapeDtypeStruct((m, n), out_dtype),
        scratch_shapes=_staging(w, k, tn),
        compiler_params=_params("parallel", "arbitrary"),
        name=name,
    )(*args)


def _mm_dilated_kernel(a_ref, w_ref, *refs, dils):
    out_refs, scratch = refs[:len(dils)], refs[len(dils):]
    res_sc = scratch[-1]
    w = _staged(w_ref, scratch[0] if len(scratch) > 1 else None)
    res = jnp.dot(a_ref[...], w[...], preferred_element_type=F32)
    n_chunks, tm, _ = res_sc.shape
    for c in range(n_chunks):
        res_sc[c] = res[:, c * LANES:(c + 1) * LANES]
    for o_ref, d in zip(out_refs, dils):
        for r in range(d):
            for c in range(n_chunks):
                o_ref[r, :, c * LANES:(c + 1) * LANES] = (
                    res_sc[c, pl.ds(r, tm // d, stride=d), :].astype(o_ref.dtype))


def matmul_dilated(a, w, layer, n, batch, seq, dils):
    m, k = a.shape
    tm, tn, w_bufs = _mm_tiles(m, n, k, 0, 1, 2 * len(dils) + 4, w.dtype.itemsize)
    assert seq % tm == 0 and tm % (16 * max(dils)) == 0
    per_batch = seq // tm
    return pl.pallas_call(
        functools.partial(_mm_dilated_kernel, dils=dils),
        grid=(n // tn, m // tm),
        in_specs=[pl.BlockSpec((tm, k), lambda j, i: (i, 0)),
                  _weight_spec(k, tn, w_bufs, layer, 0, 0)],
        out_specs=[pl.BlockSpec((None, d, tm // d, tn), lambda j, i: (i // per_batch, 0, i % per_batch, j))
                   for d in dils],
        out_shape=[jax.ShapeDtypeStruct((batch, d, seq // d, n), BF16) for d in dils],
        scratch_shapes=_staging(w, k, tn) + [pltpu.VMEM((tn // LANES, tm, LANES), F32)],
        compiler_params=_params("parallel", "arbitrary"),
        name="in_proj_a",
    )(a, w)


def _in_proj_prep_kernel(w_ref, o_ref):
    for layer in range(o_ref.shape[0]):
        o_ref[layer] = jnp.transpose(w_ref[:, layer, :]).astype(o_ref.dtype)


def in_proj_weights(w_in, d):
    depth, k, _ = w_in.shape
    _, g0, c0, f0 = _in_proj_columns(d)
    tn = 256 if g0 % 256 == 0 and (f0 - c0) % 256 == 0 else LANES
    n_ab = g0 // tn

    def source_rows(t):
        return (jnp.where(t < n_ab, t * tn, c0 + (t - n_ab) * tn), 0, 0)

    return pl.pallas_call(
        _in_proj_prep_kernel,
        grid=((g0 + f0 - c0) // tn,),
        in_specs=[pl.BlockSpec((pl.Element(tn), pl.Element(depth), pl.Element(k)), source_rows)],
        out_specs=pl.BlockSpec((depth, k, tn), lambda t: (0, 0, t)),
        out_shape=jax.ShapeDtypeStruct((depth, k, g0 + f0 - c0), BF16),
        compiler_params=_params("parallel"),
        name="in_proj_weights",
    )(jnp.transpose(w_in, (2, 0, 1)))


def swiglu(a, wg, wu, layer):
    m, k = a.shape
    n = wg.shape[2]
    tm, tn, w_bufs = _mm_tiles(m, n, k, 0, 2, 2, 4)
    w_spec = _weight_spec(k, tn, w_bufs, layer, 0, 0)
    return pl.pallas_call(
        _swiglu_kernel,
        grid=(n // tn, m // tm),
        in_specs=[pl.BlockSpec((tm, k), lambda j, i: (i, 0)), w_spec, w_spec],
        out_specs=pl.BlockSpec((tm, tn), lambda j, i: (i, j)),
        out_shape=jax.ShapeDtypeStruct((m, n), BF16),
        scratch_shapes=[pltpu.VMEM((k, tn), BF16), pltpu.VMEM((k, tn), BF16)],
        compiler_params=_params("parallel", "arbitrary"),
        name="swiglu",
    )(a, wg, wu)


def _head_rmsnorm(o, g):
    ms = jnp.mean(o * o, axis=-1, keepdims=True)
    return o * lax.rsqrt(ms + NORM_EPS) * g


def _lane_tile(x, reps):
    return x if reps == 1 else jnp.concatenate([x] * reps, axis=1)


def _with_ones(v):
    return jnp.concatenate([v, jnp.ones((v.shape[0], LANES), v.dtype)], axis=1)


def _flash_step(s, v, m_sc, l_sc, acc_sc):
    m_prev = m_sc[...]
    m_new = jnp.maximum(m_prev, jnp.max(s, axis=-1, keepdims=True))
    alpha = jnp.exp2(m_prev - m_new)
    p = jnp.exp2(s - _lane_tile(m_new, s.shape[1] // LANES))
    l_sc[...] = alpha * l_sc[...] + jnp.sum(p, axis=-1, keepdims=True)
    acc_sc[...] = alpha * acc_sc[...] + jnp.dot(p.astype(BF16), v, preferred_element_type=F32)
    m_sc[...] = m_new


def _flash_reset(m_sc, l_sc, acc_sc):
    m_sc[...] = jnp.full(m_sc.shape, NEG_INF, F32)
    l_sc[...] = jnp.zeros(l_sc.shape, F32)
    acc_sc[...] = jnp.zeros(acc_sc.shape, F32)


def _dil_kernel(q_ref, kp_ref, kc_ref, vp_ref, vc_ref, o_ref, lse_ref, s_sc, p_sc, *, dil, slopes):
    i = pl.program_id(2)
    n_heads = len(slopes)
    qi = lax.broadcasted_iota(jnp.int32, (BAND, 2 * BAND), 0)
    kj = lax.broadcasted_iota(jnp.int32, (BAND, 2 * BAND), 1)
    dist = qi + BAND - kj
    first_key = jnp.where(i > 0, 0, BAND)
    ok = (dist >= 0) & (dist <= BAND) & (kj >= first_key)
    pen = (dist * dil).astype(F32)
    lane = lax.broadcasted_iota(jnp.int32, (BAND, LANES), 1)
    lse_tile = jnp.zeros((BAND, LANES), F32)
    for h in range(n_heads):
        sl = slice(h * HEAD_DIM, (h + 1) * HEAD_DIM)
        k = jnp.concatenate([kp_ref[:, sl], kc_ref[:, sl]], axis=0)
        s_sc[h] = lax.dot_general(q_ref[:, sl], k, _NT, preferred_element_type=F32)
    for h, slope in enumerate(slopes):
        s = jnp.where(ok, s_sc[h] * SCALE - slope * pen, NEG_INF)
        m = jnp.max(s, axis=-1, keepdims=True)
        e = jnp.exp(s - m)
        den = jnp.sum(e, axis=-1, keepdims=True)
        p_sc[h] = (e * (1.0 / den)).astype(BF16)
        lse_tile = jnp.where(lane == h, m + jnp.log(den), lse_tile)
    for h in range(n_heads):
        sl = slice(h * HEAD_DIM, (h + 1) * HEAD_DIM)
        v = jnp.concatenate([vp_ref[:, sl], vc_ref[:, sl]], axis=0)
        o_ref[:, sl] = jnp.dot(p_sc[h], v, preferred_element_type=F32).astype(o_ref.dtype)
    lse_ref[...] = lse_tile


def dilated_pattern(h_d, dil, slopes):
    batch, _, n_rows, three_ha = h_d.shape
    ha = three_ha // 3
    nqb = n_rows // BAND

    def spec(col, prev):
        def index_map(b, r, i):
            return (b, r, jnp.maximum(i - 1, 0) if prev else i, col)
        return pl.BlockSpec((None, None, BAND, ha), index_map)

    n_heads = len(slopes)
    return pl.pallas_call(
        functools.partial(_dil_kernel, dil=dil, slopes=slopes),
        grid=(batch, dil, nqb),
        in_specs=[spec(0, False), spec(1, True), spec(1, False), spec(2, True), spec(2, False)],
        out_specs=[spec(0, False), pl.BlockSpec((None, None, BAND, LANES), lambda b, r, i: (b, r, i, 0))],
        out_shape=[jax.ShapeDtypeStruct((batch, dil, n_rows, ha), BF16),
                   jax.ShapeDtypeStruct((batch, dil, n_rows, LANES), F32)],
        scratch_shapes=[pltpu.VMEM((n_heads, BAND, 2 * BAND), F32), pltpu.VMEM((n_heads, BAND, 2 * BAND), BF16)],
        compiler_params=_params("parallel", "parallel", "arbitrary"),
        name=f"dilated_d{dil}",
    )(h_d, h_d, h_d, h_d, h_d)


def _dil_combine_kernel(*refs, n_heads, dils):
    n = len(dils)
    o_refs, l_refs = refs[:n], refs[n:2 * n]
    g_ref, out_ref = refs[2 * n], refs[2 * n + 1]
    scratch = refs[2 * n + 2:]
    tm = out_ref.shape[0]
    outs, lses = [], []
    for p, d in enumerate(dils):
        o_sc, l_sc = scratch[2 * p], scratch[2 * p + 1]
        for r in range(d):
            rows = pl.ds(r, tm // d, stride=d)
            for h in range(n_heads):
                o_sc[h, rows, :] = o_refs[p][r, :, h * HEAD_DIM:(h + 1) * HEAD_DIM].astype(F32)
            l_sc[rows, :] = l_refs[p][r]
        outs.append(o_sc)
        lses.append(l_sc[...])
    mx = functools.reduce(jnp.maximum, lses)
    es = [jnp.exp(l - mx) for l in lses]
    inv = 1.0 / functools.reduce(jnp.add, es)
    ws = [e * inv for e in es]
    for h in range(n_heads):
        sl = slice(h * HEAD_DIM, (h + 1) * HEAD_DIM)
        o = functools.reduce(jnp.add, [w[:, h:h + 1] * o_sc[h] for w, o_sc in zip(ws, outs)])
        out_ref[:, sl] = _head_rmsnorm(o, g_ref[:, sl]).astype(out_ref.dtype)


def dilated_combine(outs, lses, gain, dils):
    batch, _, seq, ha = outs[0].shape
    seq *= dils[0]
    tm = 256
    assert seq % tm == 0 and tm % (8 * max(dils)) == 0
    nt = seq // tm

    def spec(d, width):
        return pl.BlockSpec((None, d, tm // d, width), lambda b, i: (b, 0, i, 0))

    scratch = []
    for _ in dils:
        scratch += [pltpu.VMEM((ha // HEAD_DIM, tm, HEAD_DIM), F32), pltpu.VMEM((tm, LANES), F32)]
    return pl.pallas_call(
        functools.partial(_dil_combine_kernel, n_heads=ha // HEAD_DIM, dils=dils),
        grid=(batch, nt),
        in_specs=([spec(d, ha) for d in dils] + [spec(d, LANES) for d in dils]
                  + [pl.BlockSpec((1, ha), lambda b, i: (0, 0))]),
        out_specs=pl.BlockSpec((tm, ha), lambda b, i: (b * nt + i, 0)),
        out_shape=jax.ShapeDtypeStruct((batch * seq, ha), BF16),
        scratch_shapes=scratch,
        compiler_params=_params("parallel", "parallel"),
        name="dilated_combine",
    )(*outs, *lses, gain.reshape(1, ha))


def _compress_kernel(x_ref, pe_ref, w1_ref, w2_ref, o_ref):
    x = (x_ref[...].astype(F32) + pe_ref[...]).astype(BF16)
    hid = jnp.dot(x, w1_ref[...], preferred_element_type=F32)
    act = 0.5 * hid * (1.0 + jnp.tanh(np.sqrt(2.0 / np.pi) * (hid + 0.044715 * (hid * hid * hid))))
    o_ref[...] = jnp.dot(act.astype(BF16), w2_ref[...], preferred_element_type=F32).astype(o_ref.dtype)


def compress(blocks, pe, w1, w2):
    _, rows, width = blocks.shape
    tr = _pick(rows, (256, 128, 8))
    return pl.pallas_call(
        _compress_kernel,
        grid=(2, rows // tr),
        in_specs=[pl.BlockSpec((None, tr, width), lambda s, i: (s, i, 0)),
                  pl.BlockSpec((None, 1, width), lambda s, i: (s, 0, 0)),
                  pl.BlockSpec((None, width, HEAD_DIM), lambda s, i: (s, 0, 0)),
                  pl.BlockSpec((None, HEAD_DIM, HEAD_DIM), lambda s, i: (s, 0, 0))],
        out_specs=pl.BlockSpec((None, tr, HEAD_DIM), lambda s, i: (s, i, 0)),
        out_shape=jax.ShapeDtypeStruct((2, rows, HEAD_DIM), BF16),
        compiler_params=_params("parallel", "parallel"),
        name="nsa_compress",
    )(blocks, pe, w1, w2)


def _nsa_kernel(q_ref, kc_ref, vc_ref, ks_ref, vs_ref, kw_ref, vw_ref, ov_ref, z_ref, g_ref, o_ref,
                q_sc, sel_sc, m_sc, l_sc, acc_sc, *, tq, tk, n_cmp, n_sel, slopes):
    g = pl.program_id(1)
    i = pl.program_id(2)
    rep = len(slopes[0])
    sel_shift = int(np.log2(SEL_BLOCK))

    for r in range(rep):
        q_sc[r * tq:(r + 1) * tq, :] = q_ref[:, r * HEAD_DIM:(r + 1) * HEAD_DIM]
    head_slopes = [jnp.where(g == 0, slopes[0][r] * LOG2E, slopes[1][r] * LOG2E) for r in range(rep)]

    def biased(s, dist, ok):
        dist_f = dist.astype(F32)
        return jnp.concatenate(
            [jnp.where(ok, s[r * tq:(r + 1) * tq] * (SCALE * LOG2E) - head_slopes[r] * dist_f, NEG_INF)
             for r in range(rep)], axis=0)

    def rel_pos(k0, width):
        return (lax.broadcasted_iota(jnp.int32, (tq, width), 0)
                - lax.broadcasted_iota(jnp.int32, (tq, width), 1)) + (i * tq - k0)

    t_pos = i * tq + lax.broadcasted_iota(jnp.int32, (tq, n_cmp), 0)
    c_dist = t_pos - (lax.broadcasted_iota(jnp.int32, (tq, n_cmp), 1) * CMP_STRIDE + (CMP_BLOCK - 1))
    s = lax.dot_general(q_sc[...], kc_ref[...], _NT, preferred_element_type=F32)
    s = biased(s, c_dist, c_dist >= 0)
    e = jnp.exp2(s - jnp.max(s, axis=-1, keepdims=True))
    p = e * (1.0 / jnp.sum(e, axis=-1, keepdims=True))
    o_cmp = jnp.dot(p.astype(BF16), vc_ref[...], preferred_element_type=F32)
    any_valid = jnp.where(i * tq + lax.broadcasted_iota(jnp.int32, (tq, 1), 0) >= CMP_BLOCK - 1, 1.0, 0.0)

    p_sum = p[0:tq]
    for r in range(1, rep):
        p_sum = p_sum + p[r * tq:(r + 1) * tq]
    p_sum = p_sum * any_valid
    imp = lax.dot_general(ov_ref[...], p_sum.astype(BF16), _NT, preferred_element_type=F32)
    blk = lax.broadcasted_iota(jnp.int32, (n_sel, tq), 0)
    cur = lax.shift_right_logical(i * tq + lax.broadcasted_iota(jnp.int32, (n_sel, tq), 1), sel_shift)
    imp = jnp.where(blk <= cur, imp, -1.0)
    imp = jnp.where(blk == 0, FORCE_SCORE, jnp.where(blk >= cur - 1, jnp.where(blk <= cur, FORCE_SCORE, imp), imp))
    rank = jnp.zeros((n_sel, tq), jnp.int32)
    for c in range(n_sel):
        row = imp[c:c + 1, :]
        tie = jnp.where(blk > c, 1, 0)
        rank = rank + jnp.where(row > imp, 1, jnp.where(row == imp, tie, 0))
    sel_t = jnp.where(rank < min(N_SELECT, n_sel), 1.0, 0.0)
    sel_sc[...] = jnp.transpose(sel_t).astype(BF16)

    _flash_reset(m_sc, l_sc, acc_sc)

    def slc_block(kb, carry):
        k0 = pl.multiple_of(kb * tk, tk)
        dist = rel_pos(k0, tk)
        kblk = lax.shift_right_logical(k0 + lax.broadcasted_iota(jnp.int32, (n_sel, tk), 1), sel_shift)
        expand = jnp.where(lax.broadcasted_iota(jnp.int32, (n_sel, tk), 0) == kblk, 1.0, 0.0).astype(BF16)
        picked = jnp.dot(sel_sc[...], expand, preferred_element_type=F32)
        s = lax.dot_general(q_sc[...], ks_ref[pl.ds(k0, tk), :], _NT, preferred_element_type=F32)
        s = biased(s, dist, (dist >= 0) & (picked > 0.5))
        _flash_step(s, vs_ref[pl.ds(k0, tk), :], m_sc, l_sc, acc_sc)
        return carry

    lax.fori_loop(0, (i * tq + tq - 1) // tk + 1, slc_block, 0)
    o_slc = acc_sc[...] * (1.0 / l_sc[...])

    wk = NSA_WINDOW + tq
    w0 = pl.multiple_of(jnp.maximum(i * tq - NSA_WINDOW, 0), tq)
    dist = rel_pos(w0, wk)
    s = lax.dot_general(q_sc[...], kw_ref[pl.ds(w0, wk), :], _NT, preferred_element_type=F32)
    s = biased(s, dist, (dist >= 0) & (dist <= NSA_WINDOW))
    e = jnp.exp2(s - jnp.max(s, axis=-1, keepdims=True))
    o_win = jnp.dot(e.astype(BF16), vw_ref[pl.ds(w0, wk), :], preferred_element_type=F32)
    o_win = o_win * (1.0 / jnp.sum(e, axis=-1, keepdims=True))

    gate = 1.0 / (1.0 + jnp.exp(-z_ref[...]))
    for r in range(rep):
        rs = slice(r * tq, (r + 1) * tq)
        sl = slice(r * HEAD_DIM, (r + 1) * HEAD_DIM)
        c = r * N_NSA_BRANCHES
        o = (o_cmp[rs] * (gate[:, c:c + 1] * any_valid) + o_slc[rs] * gate[:, c + 1:c + 2]
             + o_win[rs] * gate[:, c + 2:c + 3])
        o_ref[:, sl] = _head_rmsnorm(o, g_ref[:, sl]).astype(o_ref.dtype)


def native_sparse_attention(h_b, kv_cmp, h_small, gain, batch, seq, slopes):
    m = h_b.shape[0]
    rep = len(slopes[0])
    gw = rep * HEAD_DIM
    hb = NSA_KV_GROUPS * gw
    n_cmp = seq // CMP_STRIDE
    n_sel = seq // SEL_BLOCK
    tq = 256
    tk = _pick(seq, (512, 256, 128))
    nq = seq // tq
    assert seq >= NSA_WINDOW + tq
    col = hb // HEAD_DIM

    def seq_spec(branch):
        return pl.BlockSpec((seq, HEAD_DIM), lambda b, g, i: (b, col + branch * NSA_KV_GROUPS + g))

    def cmp_spec(which):
        return pl.BlockSpec((None, n_cmp, HEAD_DIM), lambda b, g, i: (which, b * NSA_KV_GROUPS + g, 0))

    rows = rep * tq
    return pl.pallas_call(
        functools.partial(_nsa_kernel, tq=tq, tk=tk, n_cmp=n_cmp, n_sel=n_sel, slopes=slopes),
        grid=(batch, NSA_KV_GROUPS, nq),
        in_specs=[pl.BlockSpec((tq, gw), lambda b, g, i: (b * nq + i, g)),
                  cmp_spec(0), cmp_spec(1), seq_spec(2), seq_spec(3), seq_spec(4), seq_spec(5),
                  pl.BlockSpec((n_sel, n_cmp), lambda b, g, i: (0, 0)),
                  pl.BlockSpec((tq, LANES), lambda b, g, i: (b * nq + i, g)),
                  pl.BlockSpec((1, gw), lambda b, g, i: (0, g))],
        out_specs=pl.BlockSpec((tq, gw), lambda b, g, i: (b * nq + i, g)),
        out_shape=jax.ShapeDtypeStruct((m, hb), BF16),
        scratch_shapes=[pltpu.VMEM((rows, HEAD_DIM), BF16), pltpu.VMEM((tq, n_sel), BF16),
                        pltpu.VMEM((rows, LANES), F32), pltpu.VMEM((rows, LANES), F32),
                        pltpu.VMEM((rows, HEAD_DIM), F32)],
        compiler_params=_params("parallel", "parallel", "arbitrary"),
        name="native_sparse_attention",
    )(h_b, kv_cmp, kv_cmp, h_b, h_b, h_b, h_b, _overlap_matrix(seq), h_small, gain.reshape(1, hb))


def _forget_cumsum_kernel(z_ref, b_ref, c_ref, carry_sc, *, tr):
    @pl.when(pl.program_id(1) == 0)
    def _():
        carry_sc[...] = jnp.zeros(carry_sc.shape, F32)

    z = z_ref[...] + b_ref[...]
    log_f = jnp.minimum(z, 0.0) - jnp.log1p(jnp.exp(-jnp.abs(z)))
    tri = jnp.where(lax.broadcasted_iota(jnp.int32, (tr, tr), 0)
                    >= lax.broadcasted_iota(jnp.int32, (tr, tr), 1), 1.0, 0.0)
    c = jnp.dot(tri, log_f, preferred_element_type=F32, precision=lax.Precision.HIGHEST) + carry_sc[0:1, :]
    c_ref[...] = c
    carry_sc[...] = jnp.broadcast_to(c[tr - 1:tr, :], carry_sc.shape)


def forget_cumsum(h_small, tile, bias_row, batch, seq):
    m = h_small.shape[0]
    tr = _pick(seq, (256, 128))
    nb = seq // tr
    return pl.pallas_call(
        functools.partial(_forget_cumsum_kernel, tr=tr),
        grid=(batch, nb),
        in_specs=[pl.BlockSpec((tr, LANES), lambda b, i: (b * nb + i, tile)),
                  pl.BlockSpec((1, LANES), lambda b, i: (0, 0))],
        out_specs=pl.BlockSpec((tr, LANES), lambda b, i: (b * nb + i, 0)),
        out_shape=jax.ShapeDtypeStruct((m, LANES), F32),
        scratch_shapes=[pltpu.VMEM((8, LANES), F32)],
        compiler_params=_params("parallel", "arbitrary"),
        name="forget_cumsum",
    )(h_small, bias_row)


def _fox_kernel(q_ref, k_ref, v_ref, cq_ref, ck_ref, g_ref, o_ref, cq_sc, m_sc, l_sc, acc_sc,
                *, tq, tk, n_heads, hb):
    i = pl.program_id(1)
    n_full = (i * tq) // tk
    reps = tk // LANES
    for h0 in range(0, n_heads, hb):
        heads = list(range(h0, h0 + hb))
        for b, h in enumerate(heads):
            cq_sc[b] = jnp.broadcast_to(cq_ref[:, h:h + 1] * LOG2E, (tq, LANES))
            m_sc[b] = jnp.full((tq, LANES), NEG_INF, F32)
            l_sc[b] = jnp.zeros((tq, LANES), F32)
            acc_sc[b] = jnp.zeros((tq, HEAD_DIM), F32)

        def block(kb, masked):
            k0 = pl.multiple_of(kb * tk, tk)
            rows = pl.ds(k0, tk)
            qk = [lax.dot_general(q_ref[:, h * HEAD_DIM:(h + 1) * HEAD_DIM],
                                  k_ref[rows, h * HEAD_DIM:(h + 1) * HEAD_DIM], _NT,
                                  preferred_element_type=F32) for h in heads]
            if masked:
                causal = (k0 + lax.broadcasted_iota(jnp.int32, (tq, tk), 1)
                          <= i * tq + lax.broadcasted_iota(jnp.int32, (tq, tk), 0))
            ps, alphas = [], []
            for b, h in enumerate(heads):
                s = qk[b] * (SCALE * LOG2E) + _lane_tile(cq_sc[b], reps) - ck_ref[h:h + 1, rows] * LOG2E
                if masked:
                    s = jnp.where(causal, s, NEG_INF)
                m_prev = m_sc[b]
                m_new = jnp.maximum(m_prev, jnp.max(s, axis=-1, keepdims=True))
                alpha = jnp.exp2(m_prev - m_new)
                ps.append(jnp.exp2(s - _lane_tile(m_new, reps)).astype(BF16))
                m_sc[b] = m_new
                alphas.append(alpha)
            for b, h in enumerate(heads):
                pv = jnp.dot(ps[b], _with_ones(v_ref[rows, h * HEAD_DIM:(h + 1) * HEAD_DIM]),
                             preferred_element_type=F32)
                acc_sc[b] = alphas[b] * acc_sc[b] + pv[:, :HEAD_DIM]
                l_sc[b] = alphas[b] * l_sc[b] + pv[:, HEAD_DIM:]

        def full_block(kb, carry):
            block(kb, False)
            return carry

        lax.fori_loop(0, n_full, full_block, 0)
        block(n_full, True)
        for b, h in enumerate(heads):
            sl = slice(h * HEAD_DIM, (h + 1) * HEAD_DIM)
            o = acc_sc[b] * (1.0 / l_sc[b])
            o_ref[:, sl] = _head_rmsnorm(o, g_ref[:, sl]).astype(o_ref.dtype)


def forgetting_attention(h_c, c_tok, c_head, gain, batch, seq):
    m, three_hc = h_c.shape
    hc = three_hc // 3
    hp = c_head.shape[0] // batch
    tq = _pick(seq, (256, 128))
    tk = _pick(seq, (512, 256, 128))
    nq = seq // tq
    n_heads = hc // HEAD_DIM
    hb = _pick(n_heads, (4, 3, 2, 1))
    resident = functools.partial(pl.BlockSpec, pipeline_mode=pl.Buffered(1))
    return pl.pallas_call(
        functools.partial(_fox_kernel, tq=tq, tk=tk, n_heads=n_heads, hb=hb),
        grid=(batch, nq),
        in_specs=[pl.BlockSpec((tq, hc), lambda b, i: (b * nq + i, 0)),
                  resident((seq, hc), lambda b, i: (b, 1)),
                  resident((seq, hc), lambda b, i: (b, 2)),
                  pl.BlockSpec((tq, LANES), lambda b, i: (b * nq + i, 0)),
                  resident((hp, seq), lambda b, i: (b, 0)),
                  pl.BlockSpec((1, hc), lambda b, i: (0, 0))],
        out_specs=pl.BlockSpec((tq, hc), lambda b, i: (b * nq + i, 0)),
        out_shape=jax.ShapeDtypeStruct((m, hc), BF16),
        scratch_shapes=[pltpu.VMEM((hb, tq, LANES), F32), pltpu.VMEM((hb, tq, LANES), F32),
                        pltpu.VMEM((hb, tq, LANES), F32), pltpu.VMEM((hb, tq, HEAD_DIM), F32)],
        compiler_params=_params("parallel", "arbitrary"),
        name="forgetting_attention",
    )(h_c, h_c, h_c, c_tok, c_head, gain.reshape(1, hc))


def _alibi_slopes(h_dil, h_nsa):
    n = h_dil + h_nsa
    s = (2.0 ** (-8.0 * np.arange(1, n + 1) / n)).astype(np.float32)
    nsa_mask = np.isin(np.arange(n) % 5, [1, 3])
    dil = tuple(float(v) for v in s[~nsa_mask])
    nsa = s[nsa_mask].reshape(NSA_KV_GROUPS, h_nsa // NSA_KV_GROUPS)
    return dil, tuple(tuple(float(v) for v in row) for row in nsa)


def _overlap_matrix(seq):
    n_cmp = seq // CMP_STRIDE
    n_sel = seq // SEL_BLOCK
    c_start = np.arange(n_cmp) * CMP_STRIDE
    s_start = np.arange(n_sel) * SEL_BLOCK
    ov = ((c_start[None, :] < s_start[:, None] + SEL_BLOCK)
          & (c_start[None, :] + CMP_BLOCK > s_start[:, None]))
    return jnp.asarray(ov, BF16)


def _head_split(d):
    n_heads = d // HEAD_DIM
    h_dil = 3 * n_heads // 8
    h_nsa = n_heads // 4
    return h_dil, h_nsa, n_heads - h_dil - h_nsa


def _in_proj_columns(d):
    h_dil, h_nsa, h_fox = _head_split(d)
    b0 = 3 * h_dil * HEAD_DIM
    g0 = b0 + (h_nsa + 6 * NSA_KV_GROUPS) * HEAD_DIM
    c0 = g0 + h_nsa * N_NSA_BRANCHES
    f0 = c0 + 3 * h_fox * HEAD_DIM
    return b0, g0, c0, f0


def _small_projection(w_in, d):
    depth = w_in.shape[0]
    _, h_nsa, h_fox = _head_split(d)
    _, g0, _, f0 = _in_proj_columns(d)
    per_group = h_nsa // NSA_KV_GROUPS * N_NSA_BRANCHES
    tiles = []
    for g in range(NSA_KV_GROUPS):
        tiles += [w_in[:, :, g0 + g * per_group:g0 + (g + 1) * per_group],
                  jnp.zeros((depth, d, LANES - per_group), w_in.dtype)]
    tiles += [w_in[:, :, f0:f0 + h_fox], jnp.zeros((depth, d, LANES - h_fox), w_in.dtype)]
    return jnp.concatenate(tiles, axis=2)


def _layer(x, batch, seq, l, norm_mix, w_mix, w_small, b_forget, cmp_pe, cmp_w1, cmp_w2, head_norm,
           w_out, norm_ffn, w_gate, w_up, w_down):
    m, d = x.shape
    h_dil, h_nsa, h_fox = _head_split(d)
    ha, hb, hc, gkv = h_dil * HEAD_DIM, h_nsa * HEAD_DIM, h_fox * HEAD_DIM, NSA_KV_GROUPS * HEAD_DIM
    slopes_dil, slopes_nsa = _alibi_slopes(h_dil, h_nsa)
    b0, g0, _, _ = _in_proj_columns(d)
    bias_row = jnp.zeros((1, LANES), F32).at[0, :h_fox].set(b_forget.astype(F32))

    xn = rmsnorm(x, norm_mix, BF16)
    dils = tuple(d for _, d in DILATED_PATTERNS)
    h_a = matmul_dilated(xn, w_mix, l, b0, batch, seq, dils)
    h_b = matmul(xn, w_mix, l, b0, g0 - b0, BF16, name="in_proj_b")
    h_c = matmul(xn, w_mix, l, g0, 3 * hc, BF16, name="in_proj_c")
    h_small = matmul(xn, w_small, l, 0, w_small.shape[2], F32, name="in_proj_small")

    outs, lses = [], []
    for h_d, dil in zip(h_a, dils):
        o, lse = dilated_pattern(h_d, dil, slopes_dil)
        outs.append(o)
        lses.append(lse)
    o_a = dilated_combine(outs, lses, head_norm[:ha], dils)

    n_chunk = seq // CMP_STRIDE

    def cmp_blocks(col):
        a = h_b[:, col:col + gkv].reshape(batch, seq, NSA_KV_GROUPS, HEAD_DIM).transpose(0, 2, 1, 3)
        chunks = a.reshape(batch, NSA_KV_GROUPS, n_chunk, CMP_STRIDE * HEAD_DIM)
        nxt = jnp.concatenate([chunks[:, :, 1:], jnp.zeros_like(chunks[:, :, :1])], axis=2)
        return jnp.concatenate([chunks, nxt], axis=-1).reshape(batch * NSA_KV_GROUPS * n_chunk, -1)

    blocks = jnp.stack([cmp_blocks(hb), cmp_blocks(hb + gkv)])
    kv_cmp = compress(blocks, cmp_pe.reshape(2, 1, CMP_BLOCK * HEAD_DIM).astype(F32),
                      cmp_w1.astype(BF16), cmp_w2.astype(BF16))
    o_b = native_sparse_attention(h_b, kv_cmp, h_small, head_norm[ha:ha + hb], batch, seq, slopes_nsa)

    c_tok = forget_cumsum(h_small, NSA_KV_GROUPS, bias_row, batch, seq)
    hp = -(-h_fox // 8) * 8
    c_head = c_tok.reshape(batch, seq, LANES)[:, :, :h_fox].transpose(0, 2, 1)
    c_head = jnp.pad(c_head, ((0, 0), (0, hp - h_fox), (0, 0))).reshape(batch * hp, seq)
    o_c = forgetting_attention(h_c, c_tok, c_head, head_norm[ha + hb:], batch, seq)

    o = jnp.concatenate([o_a, o_b, o_c], axis=1)
    x = matmul(o, w_out, l, 0, d, F32, residual=x, name="out_proj")

    hf = rmsnorm(x, norm_ffn, BF16)
    gu = swiglu(hf, w_gate, w_up, l)
    k_blocks = 2 if gu.shape[1] % (2 * LANES) == 0 else 1
    for kb in range(k_blocks):
        x = matmul(gu, w_down, l, 0, d, F32, residual=x, k_block=kb, k_blocks=k_blocks, name="down_proj")
    return x


def kernel(x, norm_mix, w_in, b_forget, cmp_pe_k, cmp_w1_k, cmp_w2_k, cmp_pe_v, cmp_w1_v, cmp_w2_v,
           head_norm, w_out, norm_ffn, w_gate, w_up, w_down, norm_final):
    batch, seq, d = x.shape
    depth = norm_mix.shape[0]
    w_mix = in_proj_weights(w_in, d)
    w_small = _small_projection(w_in, d)
    h = x.reshape(batch * seq, d)
    for l in range(depth):
        h = _layer(h, batch, seq, l, norm_mix[l], w_mix, w_small, b_forget[l],
                   jnp.stack([cmp_pe_k[l], cmp_pe_v[l]]), jnp.stack([cmp_w1_k[l], cmp_w1_v[l]]),
                   jnp.stack([cmp_w2_k[l], cmp_w2_v[l]]), head_norm[l], w_out, norm_ffn[l],
                   w_gate, w_up, w_down)
    return rmsnorm(h, norm_final, x.dtype).reshape(batch, seq, d)
```

```python
import functools

import numpy as np
import jax
import jax.numpy as jnp
from jax import lax
from jax.experimental import pallas as pl
from jax.experimental.pallas import tpu as pltpu

F32 = jnp.float32
BF16 = jnp.bfloat16

HEAD_DIM = 128
LANES = 128
NSA_KV_GROUPS = 2
DILATED_PATTERNS = ((128, 1), (512, 4), (2048, 16))
BAND = 128
CMP_BLOCK = 32
CMP_STRIDE = 16
SEL_BLOCK = 64
N_SELECT = 16
NSA_WINDOW = 512
N_NSA_BRANCHES = 3
FORCE_SCORE = 1e9
NEG_INF = -1e30
NORM_EPS = 1e-6
SCALE = HEAD_DIM ** -0.5
LOG2E = float(np.log2(np.e))
VMEM_LIMIT = 58 * 1024 * 1024
MM_VMEM_BUDGET = 51 * 1024 * 1024

_NT = (((1,), (1,)), ((), ()))


def _params(*sem):
    return pltpu.CompilerParams(dimension_semantics=sem, vmem_limit_bytes=VMEM_LIMIT)


def _pick(n, prefs):
    for t in prefs:
        if n % t == 0:
            return t
    return n


def _rmsnorm_kernel(x_ref, g_ref, o_ref):
    x = x_ref[...].astype(F32)
    ms = jnp.mean(x * x, axis=-1, keepdims=True)
    o_ref[...] = (x * lax.rsqrt(ms + NORM_EPS) * g_ref[...]).astype(o_ref.dtype)


def rmsnorm(x, g, out_dtype):
    m, d = x.shape
    tm = _pick(m, (512, 256, 128, 8))
    return pl.pallas_call(
        _rmsnorm_kernel,
        grid=(m // tm,),
        in_specs=[pl.BlockSpec((tm, d), lambda i: (i, 0)),
                  pl.BlockSpec((1, d), lambda i: (0, 0))],
        out_specs=pl.BlockSpec((tm, d), lambda i: (i, 0)),
        out_shape=jax.ShapeDtypeStruct((m, d), out_dtype),
        compiler_params=_params("parallel"),
        name="rmsnorm",
    )(x, g.reshape(1, d).astype(F32))


def _staged(w_ref, wb_sc):
    if wb_sc is None:
        return w_ref

    @pl.when(pl.program_id(1) == 0)
    def _():
        wb_sc[...] = w_ref[...].astype(BF16)

    return wb_sc


def _mm_kernel(a_ref, w_ref, o_ref, wb_sc=None):
    w = _staged(w_ref, wb_sc)
    o_ref[...] = jnp.dot(a_ref[...], w[...], preferred_element_type=F32).astype(o_ref.dtype)


def _mm_res_kernel(a_ref, w_ref, r_ref, o_ref, wb_sc=None):
    w = _staged(w_ref, wb_sc)
    acc = jnp.dot(a_ref[...], w[...], preferred_element_type=F32)
    o_ref[...] = (r_ref[...] + acc).astype(o_ref.dtype)


def _swiglu_kernel(a_ref, wg_ref, wu_ref, o_ref, wg_sc, wu_sc):
    wg = _staged(wg_ref, wg_sc)
    wu = _staged(wu_ref, wu_sc)
    a = a_ref[...]
    g = jnp.dot(a, wg[...], preferred_element_type=F32)
    u = jnp.dot(a, wu[...], preferred_element_type=F32)
    o_ref[...] = (g * (1.0 / (1.0 + jnp.exp(-g))) * u).astype(o_ref.dtype)


def _mm_tiles(m, n, k, col0, n_weights, out_bytes, w_bytes):
    best = None
    staged = 2 if w_bytes == 4 else 0
    for tn in (1024, 512, 256, 128) + ((n,) if n < 512 and col0 == 0 else ()):
        if n % tn or col0 % tn:
            continue
        for tm in (1024, 512, 256, 128):
            if m % tm:
                continue
            rest = 2 * tm * k * 2 + 2 * tm * tn * out_bytes + (n_weights + 1) * tm * tn * 4
            for w_bufs in (2, 1):
                need = n_weights * k * tn * (w_bytes * w_bufs + staged) + rest
                if need <= MM_VMEM_BUDGET and (best is None or (tm * tn, w_bufs) > (best[0] * best[1], best[2])):
                    best = (tm, tn, w_bufs)
    assert best is not None, (m, n, k, col0)
    return best


def _weight_spec(k, tn, w_bufs, layer, k_block, col_block0):
    return pl.BlockSpec((None, k, tn), lambda j, i: (layer, k_block, col_block0 + j),
                        pipeline_mode=pl.Buffered(w_bufs))


def _staging(w, k, tn, n_weights=1):
    return [pltpu.VMEM((k, tn), BF16)] * n_weights if w.dtype == F32 else []


def matmul(a, w, layer, col0, n, out_dtype, residual=None, k_block=0, k_blocks=1, name="matmul"):
    m = a.shape[0]
    k = a.shape[1] // k_blocks
    out_bytes = jnp.dtype(out_dtype).itemsize + (0 if residual is None else residual.dtype.itemsize)
    tm, tn, w_bufs = _mm_tiles(m, n, k, col0, 1, out_bytes, w.dtype.itemsize)
    in_specs = [pl.BlockSpec((tm, k), lambda j, i: (i, k_block)),
                _weight_spec(k, tn, w_bufs, layer, k_block, col0 // tn)]
    args = [a, w]
    kern = _mm_kernel
    if residual is not None:
        in_specs.append(pl.BlockSpec((tm, tn), lambda j, i: (i, j)))
        args.append(residual)
        kern = _mm_res_kernel
    return pl.pallas_call(
        kern,
        grid=(n // tn, m // tm),
        in_specs=in_specs,
        out_specs=pl.BlockSpec((tm, tn), lambda j, i: (i, j)),
        out_shape=jax.ShapeDtypeStruct((m, n), out_dtype),
        scratch_shapes=_staging(w, k, tn),
        compiler_params=_params("parallel", "arbitrary"),
        name=name,
    )(*args)


def _mm_dilated_kernel(a_ref, w_ref, *refs, dils):
    out_refs, scratch = refs[:len(dils)], refs[len(dils):]
    res_sc = scratch[-1]
    w = _staged(w_ref, scratch[0] if len(scratch) > 1 else None)
    res = jnp.dot(a_ref[...], w[...], preferred_element_type=F32)
    n_chunks, tm, _ = res_sc.shape
    for c in range(n_chunks):
        res_sc[c] = res[:, c * LANES:(c + 1) * LANES]
    for o_ref, d in zip(out_refs, dils):
        if d == 1:
            o_ref[0] = res.astype(o_ref.dtype)
            continue
        for r in range(d):
            for c in range(n_chunks):
                o_ref[r, :, c * LANES:(c + 1) * LANES] = (
                    res_sc[c, pl.ds(r, tm // d, stride=d), :].astype(o_ref.dtype))


def matmul_dilated(a, w, layer, n, batch, seq, dils):
    m, k = a.shape
    tm, tn, w_bufs = _mm_tiles(m, n, k, 0, 1, 2 * len(dils) + 4, w.dtype.itemsize)
    assert seq % tm == 0 and tm % (16 * max(dils)) == 0
    per_batch = seq // tm
    return pl.pallas_call(
        functools.partial(_mm_dilated_kernel, dils=dils),
        grid=(n // tn, m // tm),
        in_specs=[pl.BlockSpec((tm, k), lambda j, i: (i, 0)),
                  _weight_spec(k, tn, w_bufs, layer, 0, 0)],
        out_specs=[pl.BlockSpec((None, d, tm // d, tn), lambda j, i: (i // per_batch, 0, i % per_batch, j))
                   for d in dils],
        out_shape=[jax.ShapeDtypeStruct((batch, d, seq // d, n), BF16) for d in dils],
        scratch_shapes=_staging(w, k, tn) + [pltpu.VMEM((tn // LANES, tm, LANES), F32)],
        compiler_params=_params("parallel", "arbitrary"),
        name="in_proj_a",
    )(a, w)


def _in_proj_prep_kernel(w_ref, o_ref):
    for layer in range(o_ref.shape[0]):
        o_ref[layer] = jnp.transpose(w_ref[:, layer, :]).astype(o_ref.dtype)


def in_proj_weights(w_in, d):
    depth, k, _ = w_in.shape
    _, g0, c0, f0 = _in_proj_columns(d)
    tn = 256 if g0 % 256 == 0 and (f0 - c0) % 256 == 0 else LANES
    n_ab = g0 // tn

    def source_rows(t):
        return (jnp.where(t < n_ab, t * tn, c0 + (t - n_ab) * tn), 0, 0)

    return pl.pallas_call(
        _in_proj_prep_kernel,
        grid=((g0 + f0 - c0) // tn,),
        in_specs=[pl.BlockSpec((pl.Element(tn), pl.Element(depth), pl.Element(k)), source_rows)],
        out_specs=pl.BlockSpec((depth, k, tn), lambda t: (0, 0, t)),
        out_shape=jax.ShapeDtypeStruct((depth, k, g0 + f0 - c0), BF16),
        compiler_params=_params("parallel"),
        name="in_proj_weights",
    )(jnp.transpose(w_in, (2, 0, 1)))


def swiglu(a, wg, wu, layer):
    m, k = a.shape
    n = wg.shape[2]
    tm, tn, w_bufs = _mm_tiles(m, n, k, 0, 2, 2, 4)
    w_spec = _weight_spec(k, tn, w_bufs, layer, 0, 0)
    return pl.pallas_call(
        _swiglu_kernel,
        grid=(n // tn, m // tm),
        in_specs=[pl.BlockSpec((tm, k), lambda j, i: (i, 0)), w_spec, w_spec],
        out_specs=pl.BlockSpec((tm, tn), lambda j, i: (i, j)),
        out_shape=jax.ShapeDtypeStruct((m, n), BF16),
        scratch_shapes=[pltpu.VMEM((k, tn), BF16), pltpu.VMEM((k, tn), BF16)],
        compiler_params=_params("parallel", "arbitrary"),
        name="swiglu",
    )(a, wg, wu)


def _head_rmsnorm(o, g):
    ms = jnp.mean(o * o, axis=-1, keepdims=True)
    return o * lax.rsqrt(ms + NORM_EPS) * g


def _lane_tile(x, reps):
    return x if reps == 1 else jnp.concatenate([x] * reps, axis=1)


def _with_ones(v):
    return jnp.concatenate([v, jnp.ones((v.shape[0], LANES), v.dtype)], axis=1)


def _flash_step(s, v, m_sc, l_sc, acc_sc):
    m_prev = m_sc[...]
    m_new = jnp.maximum(m_prev, jnp.max(s, axis=-1, keepdims=True))
    alpha = jnp.exp2(m_prev - m_new)
    p = jnp.exp2(s - _lane_tile(m_new, s.shape[1] // LANES))
    l_sc[...] = alpha * l_sc[...] + jnp.sum(p, axis=-1, keepdims=True)
    acc_sc[...] = alpha * acc_sc[...] + jnp.dot(p.astype(BF16), v, preferred_element_type=F32)
    m_sc[...] = m_new


def _flash_reset(m_sc, l_sc, acc_sc):
    m_sc[...] = jnp.full(m_sc.shape, NEG_INF, F32)
    l_sc[...] = jnp.zeros(l_sc.shape, F32)
    acc_sc[...] = jnp.zeros(acc_sc.shape, F32)


def _dil_kernel(q_ref, kp_ref, kc_ref, vp_ref, vc_ref, o_ref, lse_ref, s_sc, p_sc, *, dil, slopes):
    i = pl.program_id(2)
    n_heads = len(slopes)
    qi = lax.broadcasted_iota(jnp.int32, (BAND, 2 * BAND), 0)
    kj = lax.broadcasted_iota(jnp.int32, (BAND, 2 * BAND), 1)
    dist = qi + BAND - kj
    first_key = jnp.where(i > 0, 0, BAND)
    ok = (dist >= 0) & (dist <= BAND) & (kj >= first_key)
    pen = (dist * dil).astype(F32)
    lane = lax.broadcasted_iota(jnp.int32, (BAND, LANES), 1)
    lse_tile = jnp.zeros((BAND, LANES), F32)
    for h in range(n_heads):
        sl = slice(h * HEAD_DIM, (h + 1) * HEAD_DIM)
        k = jnp.concatenate([kp_ref[:, sl], kc_ref[:, sl]], axis=0)
        s_sc[h] = lax.dot_general(q_ref[:, sl], k, _NT, preferred_element_type=F32)
    for h, slope in enumerate(slopes):
        s = jnp.where(ok, s_sc[h] * SCALE - slope * pen, NEG_INF)
        m = jnp.max(s, axis=-1, keepdims=True)
        e = jnp.exp(s - m)
        den = jnp.sum(e, axis=-1, keepdims=True)
        p_sc[h] = (e * (1.0 / den)).astype(BF16)
        lse_tile = jnp.where(lane == h, m + jnp.log(den), lse_tile)
    for h in range(n_heads):
        sl = slice(h * HEAD_DIM, (h + 1) * HEAD_DIM)
        v = jnp.concatenate([vp_ref[:, sl], vc_ref[:, sl]], axis=0)
        o_ref[:, sl] = jnp.dot(p_sc[h], v, preferred_element_type=F32).astype(o_ref.dtype)
    lse_ref[...] = lse_tile


def dilated_pattern(h_d, dil, slopes):
    batch, _, n_rows, three_ha = h_d.shape
    ha = three_ha // 3
    nqb = n_rows // BAND

    def spec(col, prev):
        def index_map(b, r, i):
            return (b, r, jnp.maximum(i - 1, 0) if prev else i, col)
        return pl.BlockSpec((None, None, BAND, ha), index_map)

    n_heads = len(slopes)
    return pl.pallas_call(
        functools.partial(_dil_kernel, dil=dil, slopes=slopes),
        grid=(batch, dil, nqb),
        in_specs=[spec(0, False), spec(1, True), spec(1, False), spec(2, True), spec(2, False)],
        out_specs=[spec(0, False), pl.BlockSpec((None, None, BAND, LANES), lambda b, r, i: (b, r, i, 0))],
        out_shape=[jax.ShapeDtypeStruct((batch, dil, n_rows, ha), BF16),
                   jax.ShapeDtypeStruct((batch, dil, n_rows, LANES), F32)],
        scratch_shapes=[pltpu.VMEM((n_heads, BAND, 2 * BAND), F32), pltpu.VMEM((n_heads, BAND, 2 * BAND), BF16)],
        compiler_params=_params("parallel", "parallel", "arbitrary"),
        name=f"dilated_d{dil}",
    )(h_d, h_d, h_d, h_d, h_d)


def _dil_combine_kernel(*refs, n_heads, dils):
    n = len(dils)
    o_refs, l_refs = refs[:n], refs[n:2 * n]
    g_ref, out_ref = refs[2 * n], refs[2 * n + 1]
    scratch = refs[2 * n + 2:]
    tm = out_ref.shape[0]
    outs, lses = [], []
    for p, d in enumerate(dils):
        o_sc, l_sc = scratch[2 * p], scratch[2 * p + 1]
        for r in range(d):
            rows = pl.ds(r, tm // d, stride=d)
            for h in range(n_heads):
                o_sc[h, rows, :] = o_refs[p][r, :, h * HEAD_DIM:(h + 1) * HEAD_DIM].astype(F32)
            l_sc[rows, :] = l_refs[p][r]
        outs.append(o_sc)
        lses.append(l_sc[...])
    mx = functools.reduce(jnp.maximum, lses)
    es = [jnp.exp(l - mx) for l in lses]
    inv = 1.0 / functools.reduce(jnp.add, es)
    ws = [e * inv for e in es]
    for h in range(n_heads):
        sl = slice(h * HEAD_DIM, (h + 1) * HEAD_DIM)
        o = functools.reduce(jnp.add, [w[:, h:h + 1] * o_sc[h] for w, o_sc in zip(ws, outs)])
        out_ref[:, sl] = _head_rmsnorm(o, g_ref[:, sl]).astype(out_ref.dtype)


def dilated_combine(outs, lses, gain, dils):
    batch, _, seq, ha = outs[0].shape
    seq *= dils[0]
    tm = 256
    assert seq % tm == 0 and tm % (8 * max(dils)) == 0
    nt = seq // tm

    def spec(d, width):
        return pl.BlockSpec((None, d, tm // d, width), lambda b, i: (b, 0, i, 0))

    scratch = []
    for _ in dils:
        scratch += [pltpu.VMEM((ha // HEAD_DIM, tm, HEAD_DIM), F32), pltpu.VMEM((tm, LANES), F32)]
    return pl.pallas_call(
        functools.partial(_dil_combine_kernel, n_heads=ha // HEAD_DIM, dils=dils),
        grid=(batch, nt),
        in_specs=([spec(d, ha) for d in dils] + [spec(d, LANES) for d in dils]
                  + [pl.BlockSpec((1, ha), lambda b, i: (0, 0))]),
        out_specs=pl.BlockSpec((tm, ha), lambda b, i: (b * nt + i, 0)),
        out_shape=jax.ShapeDtypeStruct((batch * seq, ha), BF16),
        scratch_shapes=scratch,
        compiler_params=_params("parallel", "parallel"),
        name="dilated_combine",
    )(*outs, *lses, gain.reshape(1, ha))


def _compress_kernel(x_ref, pe_ref, w1_ref, w2_ref, o_ref):
    x = (x_ref[...].astype(F32) + pe_ref[...]).astype(BF16)
    hid = jnp.dot(x, w1_ref[...], preferred_element_type=F32)
    act = 0.5 * hid * (1.0 + jnp.tanh(np.sqrt(2.0 / np.pi) * (hid + 0.044715 * (hid * hid * hid))))
    o_ref[...] = jnp.dot(act.astype(BF16), w2_ref[...], preferred_element_type=F32).astype(o_ref.dtype)


def compress(blocks, pe, w1, w2):
    _, rows, width = blocks.shape
    tr = _pick(rows, (256, 128, 8))
    return pl.pallas_call(
        _compress_kernel,
        grid=(2, rows // tr),
        in_specs=[pl.BlockSpec((None, tr, width), lambda s, i: (s, i, 0)),
                  pl.BlockSpec((None, 1, width), lambda s, i: (s, 0, 0)),
                  pl.BlockSpec((None, width, HEAD_DIM), lambda s, i: (s, 0, 0)),
                  pl.BlockSpec((None, HEAD_DIM, HEAD_DIM), lambda s, i: (s, 0, 0))],
        out_specs=pl.BlockSpec((None, tr, HEAD_DIM), lambda s, i: (s, i, 0)),
        out_shape=jax.ShapeDtypeStruct((2, rows, HEAD_DIM), BF16),
        compiler_params=_params("parallel", "parallel"),
        name="nsa_compress",
    )(blocks, pe, w1, w2)


def _nsa_kernel(q_ref, kc_ref, vc_ref, ks_ref, vs_ref, kw_ref, vw_ref, ov_ref, z_ref, g_ref, o_ref,
                q_sc, sel_sc, m_sc, l_sc, acc_sc, *, tq, tk, n_cmp, n_sel, slopes):
    g = pl.program_id(1)
    i = pl.program_id(2)
    rep = len(slopes[0])
    sel_shift = int(np.log2(SEL_BLOCK))

    for r in range(rep):
        q_sc[r * tq:(r + 1) * tq, :] = q_ref[:, r * HEAD_DIM:(r + 1) * HEAD_DIM]
    head_slopes = [jnp.where(g == 0, slopes[0][r] * LOG2E, slopes[1][r] * LOG2E) for r in range(rep)]

    def biased(s, dist, ok):
        dist_f = dist.astype(F32)
        return jnp.concatenate(
            [jnp.where(ok, s[r * tq:(r + 1) * tq] * (SCALE * LOG2E) - head_slopes[r] * dist_f, NEG_INF)
             for r in range(rep)], axis=0)

    def rel_pos(k0, width):
        return (lax.broadcasted_iota(jnp.int32, (tq, width), 0)
                - lax.broadcasted_iota(jnp.int32, (tq, width), 1)) + (i * tq - k0)

    t_pos = i * tq + lax.broadcasted_iota(jnp.int32, (tq, n_cmp), 0)
    c_dist = t_pos - (lax.broadcasted_iota(jnp.int32, (tq, n_cmp), 1) * CMP_STRIDE + (CMP_BLOCK - 1))
    s = lax.dot_general(q_sc[...], kc_ref[...], _NT, preferred_element_type=F32)
    s = biased(s, c_dist, c_dist >= 0)
    e = jnp.exp2(s - jnp.max(s, axis=-1, keepdims=True))
    p = e * (1.0 / jnp.sum(e, axis=-1, keepdims=True))
    o_cmp = jnp.dot(p.astype(BF16), vc_ref[...], preferred_element_type=F32)
    any_valid = jnp.where(i * tq + lax.broadcasted_iota(jnp.int32, (tq, 1), 0) >= CMP_BLOCK - 1, 1.0, 0.0)

    p_sum = p[0:tq]
    for r in range(1, rep):
        p_sum = p_sum + p[r * tq:(r + 1) * tq]
    p_sum = p_sum * any_valid
    imp = lax.dot_general(ov_ref[...], p_sum.astype(BF16), _NT, preferred_element_type=F32)
    blk = lax.broadcasted_iota(jnp.int32, (n_sel, tq), 0)
    cur = lax.shift_right_logical(i * tq + lax.broadcasted_iota(jnp.int32, (n_sel, tq), 1), sel_shift)
    imp = jnp.where(blk <= cur, imp, -1.0)
    imp = jnp.where(blk == 0, FORCE_SCORE, jnp.where(blk >= cur - 1, jnp.where(blk <= cur, FORCE_SCORE, imp), imp))
    rank = jnp.zeros((n_sel, tq), jnp.int32)
    for c in range(n_sel):
        row = imp[c:c + 1, :]
        tie = jnp.where(blk > c, 1, 0)
        rank = rank + jnp.where(row > imp, 1, jnp.where(row == imp, tie, 0))
    sel_t = jnp.where(rank < min(N_SELECT, n_sel), 1.0, 0.0)
    sel_sc[...] = jnp.transpose(sel_t).astype(BF16)

    _flash_reset(m_sc, l_sc, acc_sc)

    def slc_block(kb, carry):
        k0 = pl.multiple_of(kb * tk, tk)
        dist = rel_pos(k0, tk)
        kblk = lax.shift_right_logical(k0 + lax.broadcasted_iota(jnp.int32, (n_sel, tk), 1), sel_shift)
        expand = jnp.where(lax.broadcasted_iota(jnp.int32, (n_sel, tk), 0) == kblk, 1.0, 0.0).astype(BF16)
        picked = jnp.dot(sel_sc[...], expand, preferred_element_type=F32)
        s = lax.dot_general(q_sc[...], ks_ref[pl.ds(k0, tk), :], _NT, preferred_element_type=F32)
        s = biased(s, dist, (dist >= 0) & (picked > 0.5))
        _flash_step(s, vs_ref[pl.ds(k0, tk), :], m_sc, l_sc, acc_sc)
        return carry

    lax.fori_loop(0, (i * tq + tq - 1) // tk + 1, slc_block, 0)
    o_slc = acc_sc[...] * (1.0 / l_sc[...])

    wk = NSA_WINDOW + tq
    w0 = pl.multiple_of(jnp.maximum(i * tq - NSA_WINDOW, 0), tq)
    dist = rel_pos(w0, wk)
    s = lax.dot_general(q_sc[...], kw_ref[pl.ds(w0, wk), :], _NT, preferred_element_type=F32)
    s = biased(s, dist, (dist >= 0) & (dist <= NSA_WINDOW))
    e = jnp.exp2(s - jnp.max(s, axis=-1, keepdims=True))
    o_win = jnp.dot(e.astype(BF16), vw_ref[pl.ds(w0, wk), :], preferred_element_type=F32)
    o_win = o_win * (1.0 / jnp.sum(e, axis=-1, keepdims=True))

    gate = 1.0 / (1.0 + jnp.exp(-z_ref[...]))
    for r in range(rep):
        rs = slice(r * tq, (r + 1) * tq)
        sl = slice(r * HEAD_DIM, (r + 1) * HEAD_DIM)
        c = r * N_NSA_BRANCHES
        o = (o_cmp[rs] * (gate[:, c:c + 1] * any_valid) + o_slc[rs] * gate[:, c + 1:c + 2]
             + o_win[rs] * gate[:, c + 2:c + 3])
        o_ref[:, sl] = _head_rmsnorm(o, g_ref[:, sl]).astype(o_ref.dtype)


def native_sparse_attention(h_b, kv_cmp, h_small, gain, batch, seq, slopes):
    m = h_b.shape[0]
    rep = len(slopes[0])
    gw = rep * HEAD_DIM
    hb = NSA_KV_GROUPS * gw
    n_cmp = seq // CMP_STRIDE
    n_sel = seq // SEL_BLOCK
    tq = 256
    tk = _pick(seq, (512, 256, 128))
    nq = seq // tq
    assert seq >= NSA_WINDOW + tq
    col = hb // HEAD_DIM

    def seq_spec(branch):
        return pl.BlockSpec((seq, HEAD_DIM), lambda b, g, i: (b, col + branch * NSA_KV_GROUPS + g))

    def cmp_spec(which):
        return pl.BlockSpec((None, n_cmp, HEAD_DIM), lambda b, g, i: (which, b * NSA_KV_GROUPS + g, 0))

    rows = rep * tq
    return pl.pallas_call(
        functools.partial(_nsa_kernel, tq=tq, tk=tk, n_cmp=n_cmp, n_sel=n_sel, slopes=slopes),
        grid=(batch, NSA_KV_GROUPS, nq),
        in_specs=[pl.BlockSpec((tq, gw), lambda b, g, i: (b * nq + i, g)),
                  cmp_spec(0), cmp_spec(1), seq_spec(2), seq_spec(3), seq_spec(4), seq_spec(5),
                  pl.BlockSpec((n_sel, n_cmp), lambda b, g, i: (0, 0)),
                  pl.BlockSpec((tq, LANES), lambda b, g, i: (b * nq + i, g)),
                  pl.BlockSpec((1, gw), lambda b, g, i: (0, g))],
        out_specs=pl.BlockSpec((tq, gw), lambda b, g, i: (b * nq + i, g)),
        out_shape=jax.ShapeDtypeStruct((m, hb), BF16),
        scratch_shapes=[pltpu.VMEM((rows, HEAD_DIM), BF16), pltpu.VMEM((tq, n_sel), BF16),
                        pltpu.VMEM((rows, LANES), F32), pltpu.VMEM((rows, LANES), F32),
                        pltpu.VMEM((rows, HEAD_DIM), F32)],
        compiler_params=_params("parallel", "parallel", "arbitrary"),
        name="native_sparse_attention",
    )(h_b, kv_cmp, kv_cmp, h_b, h_b, h_b, h_b, _overlap_matrix(seq), h_small, gain.reshape(1, hb))


def _forget_cumsum_kernel(z_ref, b_ref, c_ref, carry_sc, *, tr):
    @pl.when(pl.program_id(1) == 0)
    def _():
        carry_sc[...] = jnp.zeros(carry_sc.shape, F32)

    z = z_ref[...] + b_ref[...]
    log_f = jnp.minimum(z, 0.0) - jnp.log1p(jnp.exp(-jnp.abs(z)))
    tri = jnp.where(lax.broadcasted_iota(jnp.int32, (tr, tr), 0)
                    >= lax.broadcasted_iota(jnp.int32, (tr, tr), 1), 1.0, 0.0)
    c = jnp.dot(tri, log_f, preferred_element_type=F32, precision=lax.Precision.HIGHEST) + carry_sc[0:1, :]
    c_ref[...] = c
    carry_sc[...] = jnp.broadcast_to(c[tr - 1:tr, :], carry_sc.shape)


def forget_cumsum(h_small, tile, bias_row, batch, seq):
    m = h_small.shape[0]
    tr = _pick(seq, (256, 128))
    nb = seq // tr
    return pl.pallas_call(
        functools.partial(_forget_cumsum_kernel, tr=tr),
        grid=(batch, nb),
        in_specs=[pl.BlockSpec((tr, LANES), lambda b, i: (b * nb + i, tile)),
                  pl.BlockSpec((1, LANES), lambda b, i: (0, 0))],
        out_specs=pl.BlockSpec((tr, LANES), lambda b, i: (b * nb + i, 0)),
        out_shape=jax.ShapeDtypeStruct((m, LANES), F32),
        scratch_shapes=[pltpu.VMEM((8, LANES), F32)],
        compiler_params=_params("parallel", "arbitrary"),
        name="forget_cumsum",
    )(h_small, bias_row)


def _fox_kernel(q_ref, k_ref, v_ref, cq_ref, ck_ref, g_ref, o_ref, cq_sc, m_sc, l_sc, acc_sc,
                *, tq, tk, n_heads, hb):
    i = pl.program_id(1)
    n_full = (i * tq) // tk
    reps = tk // LANES
    for h0 in range(0, n_heads, hb):
        heads = list(range(h0, h0 + hb))
        for b, h in enumerate(heads):
            cq_sc[b] = jnp.broadcast_to(cq_ref[:, h:h + 1] * LOG2E, (tq, LANES))
            m_sc[b] = jnp.full((tq, LANES), NEG_INF, F32)
            l_sc[b] = jnp.zeros((tq, LANES), F32)
            acc_sc[b] = jnp.zeros((tq, HEAD_DIM), F32)

        def block(kb, masked):
            k0 = pl.multiple_of(kb * tk, tk)
            rows = pl.ds(k0, tk)
            qk = [lax.dot_general(q_ref[:, h * HEAD_DIM:(h + 1) * HEAD_DIM],
                                  k_ref[rows, h * HEAD_DIM:(h + 1) * HEAD_DIM], _NT,
                                  preferred_element_type=F32) for h in heads]
            if masked:
                causal = (k0 + lax.broadcasted_iota(jnp.int32, (tq, tk), 1)
                          <= i * tq + lax.broadcasted_iota(jnp.int32, (tq, tk), 0))
            for b, h in enumerate(heads):
                s = qk[b] * (SCALE * LOG2E) + _lane_tile(cq_sc[b], reps) - ck_ref[h:h + 1, rows] * LOG2E
                if masked:
                    s = jnp.where(causal, s, NEG_INF)
                m_prev = m_sc[b]
                m_new = jnp.maximum(m_prev, jnp.max(s, axis=-1, keepdims=True))
                alpha = jnp.exp2(m_prev - m_new)
                p = jnp.exp2(s - _lane_tile(m_new, reps)).astype(BF16)
                m_sc[b] = m_new
                pv = jnp.dot(p, _with_ones(v_ref[rows, h * HEAD_DIM:(h + 1) * HEAD_DIM]),
                             preferred_element_type=F32)
                acc_sc[b] = alpha * acc_sc[b] + pv[:, :HEAD_DIM]
                l_sc[b] = alpha * l_sc[b] + pv[:, HEAD_DIM:]

        def full_block(kb, carry):
            block(kb, False)
            return carry

        lax.fori_loop(0, n_full, full_block, 0)
        block(n_full, True)
        for b, h in enumerate(heads):
            sl = slice(h * HEAD_DIM, (h + 1) * HEAD_DIM)
            o = acc_sc[b] * (1.0 / l_sc[b])
            o_ref[:, sl] = _head_rmsnorm(o, g_ref[:, sl]).astype(o_ref.dtype)


def forgetting_attention(h_c, c_tok, c_head, gain, batch, seq):
    m, three_hc = h_c.shape
    hc = three_hc // 3
    hp = c_head.shape[0] // batch
    tq = _pick(seq, (512, 256, 128))
    tk = _pick(seq, (512, 256, 128))
    nq = seq // tq
    n_heads = hc // HEAD_DIM
    hb = _pick(n_heads, (4, 3, 2, 1))
    resident = functools.partial(pl.BlockSpec, pipeline_mode=pl.Buffered(1))
    return pl.pallas_call(
        functools.partial(_fox_kernel, tq=tq, tk=tk, n_heads=n_heads, hb=hb),
        grid=(batch, nq),
        in_specs=[pl.BlockSpec((tq, hc), lambda b, i: (b * nq + i, 0)),
                  resident((seq, hc), lambda b, i: (b, 1)),
                  resident((seq, hc), lambda b, i: (b, 2)),
                  pl.BlockSpec((tq, LANES), lambda b, i: (b * nq + i, 0)),
                  resident((hp, seq), lambda b, i: (b, 0)),
                  pl.BlockSpec((1, hc), lambda b, i: (0, 0))],
        out_specs=pl.BlockSpec((tq, hc), lambda b, i: (b * nq + i, 0)),
        out_shape=jax.ShapeDtypeStruct((m, hc), BF16),
        scratch_shapes=[pltpu.VMEM((hb, tq, LANES), F32), pltpu.VMEM((hb, tq, LANES), F32),
                        pltpu.VMEM((hb, tq, LANES), F32), pltpu.VMEM((hb, tq, HEAD_DIM), F32)],
        compiler_params=_params("parallel", "arbitrary"),
        name="forgetting_attention",
    )(h_c, h_c, h_c, c_tok, c_head, gain.reshape(1, hc))


def _alibi_slopes(h_dil, h_nsa):
    n = h_dil + h_nsa
    s = (2.0 ** (-8.0 * np.arange(1, n + 1) / n)).astype(np.float32)
    nsa_mask = np.isin(np.arange(n) % 5, [1, 3])
    dil = tuple(float(v) for v in s[~nsa_mask])
    nsa = s[nsa_mask].reshape(NSA_KV_GROUPS, h_nsa // NSA_KV_GROUPS)
    return dil, tuple(tuple(float(v) for v in row) for row in nsa)


def _overlap_matrix(seq):
    n_cmp = seq // CMP_STRIDE
    n_sel = seq // SEL_BLOCK
    c_start = np.arange(n_cmp) * CMP_STRIDE
    s_start = np.arange(n_sel) * SEL_BLOCK
    ov = ((c_start[None, :] < s_start[:, None] + SEL_BLOCK)
          & (c_start[None, :] + CMP_BLOCK > s_start[:, None]))
    return jnp.asarray(ov, BF16)


def _head_split(d):
    n_heads = d // HEAD_DIM
    h_dil = 3 * n_heads // 8
    h_nsa = n_heads // 4
    return h_dil, h_nsa, n_heads - h_dil - h_nsa


def _in_proj_columns(d):
    h_dil, h_nsa, h_fox = _head_split(d)
    b0 = 3 * h_dil * HEAD_DIM
    g0 = b0 + (h_nsa + 6 * NSA_KV_GROUPS) * HEAD_DIM
    c0 = g0 + h_nsa * N_NSA_BRANCHES
    f0 = c0 + 3 * h_fox * HEAD_DIM
    return b0, g0, c0, f0


def _small_projection(w_in, d):
    depth = w_in.shape[0]
    _, h_nsa, h_fox = _head_split(d)
    _, g0, _, f0 = _in_proj_columns(d)
    per_group = h_nsa // NSA_KV_GROUPS * N_NSA_BRANCHES
    tiles = []
    for g in range(NSA_KV_GROUPS):
        tiles += [w_in[:, :, g0 + g * per_group:g0 + (g + 1) * per_group],
                  jnp.zeros((depth, d, LANES - per_group), w_in.dtype)]
    tiles += [w_in[:, :, f0:f0 + h_fox], jnp.zeros((depth, d, LANES - h_fox), w_in.dtype)]
    return jnp.concatenate(tiles, axis=2)


def _layer(x, batch, seq, l, norm_mix, w_mix, w_small, b_forget, cmp_pe, cmp_w1, cmp_w2, head_norm,
           w_out, norm_ffn, w_gate, w_up, w_down):
    m, d = x.shape
    h_dil, h_nsa, h_fox = _head_split(d)
    ha, hb, hc, gkv = h_dil * HEAD_DIM, h_nsa * HEAD_DIM, h_fox * HEAD_DIM, NSA_KV_GROUPS * HEAD_DIM
    slopes_dil, slopes_nsa = _alibi_slopes(h_dil, h_nsa)
    b0, g0, _, _ = _in_proj_columns(d)
    bias_row = jnp.zeros((1, LANES), F32).at[0, :h_fox].set(b_forget.astype(F32))

    xn = rmsnorm(x, norm_mix, BF16)
    dils = tuple(d for _, d in DILATED_PATTERNS)
    h_a = matmul_dilated(xn, w_mix, l, b0, batch, seq, dils)
    h_b = matmul(xn, w_mix, l, b0, g0 - b0, BF16, name="in_proj_b")
    h_c = matmul(xn, w_mix, l, g0, 3 * hc, BF16, name="in_proj_c")
    h_small = matmul(xn, w_small, l, 0, w_small.shape[2], F32, name="in_proj_small")

    outs, lses = [], []
    for h_d, dil in zip(h_a, dils):
        o, lse = dilated_pattern(h_d, dil, slopes_dil)
        outs.append(o)
        lses.append(lse)
    o_a = dilated_combine(outs, lses, head_norm[:ha], dils)

    n_chunk = seq // CMP_STRIDE

    def cmp_blocks(col):
        a = h_b[:, col:col + gkv].reshape(batch, seq, NSA_KV_GROUPS, HEAD_DIM).transpose(0, 2, 1, 3)
        chunks = a.reshape(batch, NSA_KV_GROUPS, n_chunk, CMP_STRIDE * HEAD_DIM)
        nxt = jnp.concatenate([chunks[:, :, 1:], jnp.zeros_like(chunks[:, :, :1])], axis=2)
        return jnp.concatenate([chunks, nxt], axis=-1).reshape(batch * NSA_KV_GROUPS * n_chunk, -1)

    blocks = jnp.stack([cmp_blocks(hb), cmp_blocks(hb + gkv)])
    kv_cmp = compress(blocks, cmp_pe.reshape(2, 1, CMP_BLOCK * HEAD_DIM).astype(F32),
                      cmp_w1.astype(BF16), cmp_w2.astype(BF16))
    o_b = native_sparse_attention(h_b, kv_cmp, h_small, head_norm[ha:ha + hb], batch, seq, slopes_nsa)

    c_tok = forget_cumsum(h_small, NSA_KV_GROUPS, bias_row, batch, seq)
    hp = -(-h_fox // 8) * 8
    c_head = c_tok.reshape(batch, seq, LANES)[:, :, :h_fox].transpose(0, 2, 1)
    c_head = jnp.pad(c_head, ((0, 0), (0, hp - h_fox), (0, 0))).reshape(batch * hp, seq)
    o_c = forgetting_attention(h_c, c_tok, c_head, head_norm[ha + hb:], batch, seq)

    o = jnp.concatenate([o_a, o_b, o_c], axis=1)
    x = matmul(o, w_out, l, 0, d, F32, residual=x, name="out_proj")

    hf = rmsnorm(x, norm_ffn, BF16)
    gu = swiglu(hf, w_gate, w_up, l)
    k_blocks = 2 if gu.shape[1] % (2 * LANES) == 0 else 1
    for kb in range(k_blocks):
        x = matmul(gu, w_down, l, 0, d, F32, residual=x, k_block=kb, k_blocks=k_blocks, name="down_proj")
    return x


def kernel(x, norm_mix, w_in, b_forget, cmp_pe_k, cmp_w1_k, cmp_w2_k, cmp_pe_v, cmp_w1_v, cmp_w2_v,
           head_norm, w_out, norm_ffn, w_gate, w_up, w_down, norm_final):
    batch, seq, d = x.shape
    depth = norm_mix.shape[0]
    w_mix = in_proj_weights(w_in, d)
    w_small = _small_projection(w_in, d)
    h = x.reshape(batch * seq, d)
    for l in range(depth):
        h = _layer(h, batch, seq, l, norm_mix[l], w_mix, w_small, b_forget[l],
                   jnp.stack([cmp_pe_k[l], cmp_pe_v[l]]), jnp.stack([cmp_w1_k[l], cmp_w1_v[l]]),
                   jnp.stack([cmp_w2_k[l], cmp_w2_v[l]]), head_norm[l], w_out, norm_ffn[l],
                   w_gate, w_up, w_down)
    return rmsnorm(h, norm_final, x.dtype).reshape(batch, seq, d)
```

```python
import functools

import numpy as np
import jax
import jax.numpy as jnp
from jax import lax
from jax.experimental import pallas as pl
from jax.experimental.pallas import tpu as pltpu

F32 = jnp.float32
BF16 = jnp.bfloat16

HEAD_DIM = 128
LANES = 128
SUBLANES = 8
BF16_SUBLANES = 16
NSA_KV_GROUPS = 2
DILATED_PATTERNS = ((128, 1), (512, 4), (2048, 16))
BAND = 128
CMP_BLOCK = 32
CMP_STRIDE = 16
SEL_BLOCK = 64
N_SELECT = 16
NSA_WINDOW = 512
N_NSA_BRANCHES = 3
FORCE_SCORE = 1e9
NEG_INF = -1e30
NORM_EPS = 1e-6
SCALE = HEAD_DIM ** -0.5
LOG2E = float(np.log2(np.e))
VMEM_LIMIT = 58 * 1024 * 1024
MM_VMEM_BUDGET = 51 * 1024 * 1024

_NT = (((1,), (1,)), ((), ()))


def _params(*sem):
    return pltpu.CompilerParams(dimension_semantics=sem, vmem_limit_bytes=VMEM_LIMIT)


def _pick(n, prefs):
    for t in prefs:
        if n % t == 0:
            return t
    return n


def _rmsnorm_kernel(x_ref, g_ref, o_ref):
    x = x_ref[...].astype(F32)
    ms = jnp.mean(x * x, axis=-1, keepdims=True)
    o_ref[...] = (x * lax.rsqrt(ms + NORM_EPS) * g_ref[...]).astype(o_ref.dtype)


def rmsnorm(x, g, out_dtype):
    m, d = x.shape
    tm = _pick(m, (512, 256, 128, SUBLANES))
    return pl.pallas_call(
        _rmsnorm_kernel,
        grid=(m // tm,),
        in_specs=[pl.BlockSpec((tm, d), lambda i: (i, 0)),
                  pl.BlockSpec((1, d), lambda i: (0, 0))],
        out_specs=pl.BlockSpec((tm, d), lambda i: (i, 0)),
        out_shape=jax.ShapeDtypeStruct((m, d), out_dtype),
        compiler_params=_params("parallel"),
        name="rmsnorm",
    )(x, g.reshape(1, d).astype(F32))


def _staged(w_ref, wb_sc):
    if wb_sc is None:
        return w_ref

    @pl.when(pl.program_id(1) == 0)
    def _():
        wb_sc[...] = w_ref[...].astype(BF16)

    return wb_sc


def _mm_kernel(a_ref, w_ref, o_ref, wb_sc=None):
    w = _staged(w_ref, wb_sc)
    o_ref[...] = jnp.dot(a_ref[...], w[...], preferred_element_type=F32).astype(o_ref.dtype)


def _mm_res_kernel(a_ref, w_ref, r_ref, o_ref, wb_sc=None):
    w = _staged(w_ref, wb_sc)
    acc = jnp.dot(a_ref[...], w[...], preferred_element_type=F32)
    o_ref[...] = (r_ref[...] + acc).astype(o_ref.dtype)


def _swiglu_kernel(a_ref, wg_ref, wu_ref, o_ref, wg_sc, wu_sc):
    wg = _staged(wg_ref, wg_sc)
    wu = _staged(wu_ref, wu_sc)
    a = a_ref[...]
    g = jnp.dot(a, wg[...], preferred_element_type=F32)
    u = jnp.dot(a, wu[...], preferred_element_type=F32)
    o_ref[...] = (g * (1.0 / (1.0 + jnp.exp(-g))) * u).astype(o_ref.dtype)


def _mm_tiles(m, n, k, col0, n_weights, out_bytes, w_bytes):
    best = None
    staged = 2 if w_bytes == 4 else 0
    for tn in (1024, 512, 256, 128) + ((n,) if n < 512 and col0 == 0 else ()):
        if n % tn or col0 % tn:
            continue
        for tm in (1024, 512, 256, 128):
            if m % tm:
                continue
            rest = 2 * tm * k * 2 + 2 * tm * tn * out_bytes + (n_weights + 1) * tm * tn * 4
            for w_bufs in (2, 1):
                need = n_weights * k * tn * (w_bytes * w_bufs + staged) + rest
                if need <= MM_VMEM_BUDGET and (best is None or (tm * tn, w_bufs) > (best[0] * best[1], best[2])):
                    best = (tm, tn, w_bufs)
    assert best is not None, (m, n, k, col0)
    return best


def _weight_spec(k, tn, w_bufs, layer, k_block, col_block0):
    return pl.BlockSpec((None, k, tn), lambda j, i: (layer, k_block, col_block0 + j),
                        pipeline_mode=pl.Buffered(w_bufs))


def _staging(w, k, tn, n_weights=1):
    return [pltpu.VMEM((k, tn), BF16)] * n_weights if w.dtype == F32 else []


def matmul(a, w, layer, col0, n, out_dtype, residual=None, k_block=0, k_blocks=1, name="matmul"):
    m = a.shape[0]
    k = a.shape[1] // k_blocks
    out_bytes = jnp.dtype(out_dtype).itemsize + (0 if residual is None else residual.dtype.itemsize)
    tm, tn, w_bufs = _mm_tiles(m, n, k, col0, 1, out_bytes, w.dtype.itemsize)
    in_specs = [pl.BlockSpec((tm, k), lambda j, i: (i, k_block)),
                _weight_spec(k, tn, w_bufs, layer, k_block, col0 // tn)]
    args = [a, w]
    kern = _mm_kernel
    if residual is not None:
        in_specs.append(pl.BlockSpec((tm, tn), lambda j, i: (i, j)))
        args.append(residual)
        kern = _mm_res_kernel
    return pl.pallas_call(
        kern,
        grid=(n // tn, m // tm),
        in_specs=in_specs,
        out_specs=pl.BlockSpec((tm, tn), lambda j, i: (i, j)),
        out_shape=jax.ShapeDtypeStruct((m, n), out_dtype),
        scratch_shapes=_staging(w, k, tn),
        compiler_params=_params("parallel", "arbitrary"),
        name=name,
    )(*args)


def _mm_dilated_kernel(a_ref, w_ref, *refs, dils):
    out_refs, scratch = refs[:len(dils)], refs[len(dils):]
    res_sc = scratch[-1]
    w = _staged(w_ref, scratch[0] if len(scratch) > 1 else None)
    res = jnp.dot(a_ref[...], w[...], preferred_element_type=F32)
    n_chunks, tm, _ = res_sc.shape
    for c in range(n_chunks):
        res_sc[c] = res[:, c * LANES:(c + 1) * LANES]
    for o_ref, d in zip(out_refs, dils):
        if d == 1:
            o_ref[0] = res.astype(o_ref.dtype)
            continue
        for r in range(d):
            for c in range(n_chunks):
                o_ref[r, :, c * LANES:(c + 1) * LANES] = (
                    res_sc[c, pl.ds(r, tm // d, stride=d), :].astype(o_ref.dtype))


def matmul_dilated(a, w, layer, n, batch, seq, dils):
    m, k = a.shape
    tm, tn, w_bufs = _mm_tiles(m, n, k, 0, 1, 2 * len(dils) + 4, w.dtype.itemsize)
    assert seq % tm == 0 and tm % (BF16_SUBLANES * max(dils)) == 0
    per_batch = seq // tm
    return pl.pallas_call(
        functools.partial(_mm_dilated_kernel, dils=dils),
        grid=(n // tn, m // tm),
        in_specs=[pl.BlockSpec((tm, k), lambda j, i: (i, 0)),
                  _weight_spec(k, tn, w_bufs, layer, 0, 0)],
        out_specs=[pl.BlockSpec((None, d, tm // d, tn), lambda j, i: (i // per_batch, 0, i % per_batch, j))
                   for d in dils],
        out_shape=[jax.ShapeDtypeStruct((batch, d, seq // d, n), BF16) for d in dils],
        scratch_shapes=_staging(w, k, tn) + [pltpu.VMEM((tn // LANES, tm, LANES), F32)],
        compiler_params=_params("parallel", "arbitrary"),
        name="in_proj_a",
    )(a, w)


def _in_proj_prep_kernel(w_ref, o_ref):
    for layer in range(o_ref.shape[0]):
        o_ref[layer] = jnp.transpose(w_ref[:, layer, :]).astype(o_ref.dtype)


def in_proj_weights(w_in, d):
    depth, k, _ = w_in.shape
    _, g0, c0, f0 = _in_proj_columns(d)
    tn = 256 if g0 % 256 == 0 and (f0 - c0) % 256 == 0 else LANES
    n_ab = g0 // tn

    def source_rows(t):
        return (jnp.where(t < n_ab, t * tn, c0 + (t - n_ab) * tn), 0, 0)

    return pl.pallas_call(
        _in_proj_prep_kernel,
        grid=((g0 + f0 - c0) // tn,),
        in_specs=[pl.BlockSpec((pl.Element(tn), pl.Element(depth), pl.Element(k)), source_rows)],
        out_specs=pl.BlockSpec((depth, k, tn), lambda t: (0, 0, t)),
        out_shape=jax.ShapeDtypeStruct((depth, k, g0 + f0 - c0), BF16),
        compiler_params=_params("parallel"),
        name="in_proj_weights",
    )(jnp.transpose(w_in, (2, 0, 1)))


def swiglu(a, wg, wu, layer):
    m, k = a.shape
    n = wg.shape[2]
    tm, tn, w_bufs = _mm_tiles(m, n, k, 0, 2, 2, 4)
    w_spec = _weight_spec(k, tn, w_bufs, layer, 0, 0)
    return pl.pallas_call(
        _swiglu_kernel,
        grid=(n // tn, m // tm),
        in_specs=[pl.BlockSpec((tm, k), lambda j, i: (i, 0)), w_spec, w_spec],
        out_specs=pl.BlockSpec((tm, tn), lambda j, i: (i, j)),
        out_shape=jax.ShapeDtypeStruct((m, n), BF16),
        scratch_shapes=[pltpu.VMEM((k, tn), BF16), pltpu.VMEM((k, tn), BF16)],
        compiler_params=_params("parallel", "arbitrary"),
        name="swiglu",
    )(a, wg, wu)


def _head_rmsnorm(o, g):
    ms = jnp.mean(o * o, axis=-1, keepdims=True)
    return o * lax.rsqrt(ms + NORM_EPS) * g


def _lane_tile(x, reps):
    return x if reps == 1 else jnp.concatenate([x] * reps, axis=1)


def _with_ones(v):
    return jnp.concatenate([v, jnp.ones((v.shape[0], LANES), v.dtype)], axis=1)


def _flash_step(s, v, m_sc, l_sc, acc_sc):
    m_prev = m_sc[...]
    m_new = jnp.maximum(m_prev, jnp.max(s, axis=-1, keepdims=True))
    alpha = jnp.exp2(m_prev - m_new)
    p = jnp.exp2(s - _lane_tile(m_new, s.shape[1] // LANES))
    l_sc[...] = alpha * l_sc[...] + jnp.sum(p, axis=-1, keepdims=True)
    acc_sc[...] = alpha * acc_sc[...] + jnp.dot(p.astype(BF16), v, preferred_element_type=F32)
    m_sc[...] = m_new


def _flash_reset(m_sc, l_sc, acc_sc):
    m_sc[...] = jnp.full(m_sc.shape, NEG_INF, F32)
    l_sc[...] = jnp.zeros(l_sc.shape, F32)
    acc_sc[...] = jnp.zeros(acc_sc.shape, F32)


def _dil_kernel(q_ref, kp_ref, kc_ref, vp_ref, vc_ref, o_ref, lse_ref, s_sc, p_sc, *, dil, slopes, qb):
    i = pl.program_id(2)
    n_heads = len(slopes)
    qi = lax.broadcasted_iota(jnp.int32, (BAND, 2 * BAND), 0)
    kj = lax.broadcasted_iota(jnp.int32, (BAND, 2 * BAND), 1)
    dist = qi + BAND - kj
    in_band = (dist >= 0) & (dist <= BAND)
    pen = (dist * dil).astype(F32)
    lane = lax.broadcasted_iota(jnp.int32, (BAND, LANES), 1)

    def keys(prev_ref, own_ref, u, sl):
        if u == 0:
            return jnp.concatenate([prev_ref[:, sl], own_ref[0:BAND, sl]], axis=0)
        return own_ref[(u - 1) * BAND:(u + 1) * BAND, sl]

    for u in range(qb):
        for h in range(n_heads):
            sl = slice(h * HEAD_DIM, (h + 1) * HEAD_DIM)
            s_sc[u * n_heads + h] = lax.dot_general(q_ref[u * BAND:(u + 1) * BAND, sl], keys(kp_ref, kc_ref, u, sl),
                                                    _NT, preferred_element_type=F32)
    for u in range(qb):
        ok = in_band
        if u == 0:
            ok = ok & (kj >= jnp.where(i > 0, 0, BAND))
        lse_tile = jnp.zeros((BAND, LANES), F32)
        for h, slope in enumerate(slopes):
            s = jnp.where(ok, s_sc[u * n_heads + h] * SCALE - slope * pen, NEG_INF)
            m = jnp.max(s, axis=-1, keepdims=True)
            e = jnp.exp(s - m)
            den = jnp.sum(e, axis=-1, keepdims=True)
            p_sc[u * n_heads + h] = (e * (1.0 / den)).astype(BF16)
            lse_tile = jnp.where(lane == h, m + jnp.log(den), lse_tile)
        lse_ref[u * BAND:(u + 1) * BAND, :] = lse_tile
    for u in range(qb):
        for h in range(n_heads):
            sl = slice(h * HEAD_DIM, (h + 1) * HEAD_DIM)
            o = jnp.dot(p_sc[u * n_heads + h], keys(vp_ref, vc_ref, u, sl), preferred_element_type=F32)
            o_ref[u * BAND:(u + 1) * BAND, sl] = o.astype(o_ref.dtype)


def dilated_pattern(h_d, dil, slopes):
    batch, _, n_rows, three_ha = h_d.shape
    ha = three_ha // 3
    nqb = n_rows // BAND
    qb = 2 if nqb % 2 == 0 else 1

    def own(col):
        return pl.BlockSpec((None, None, qb * BAND, ha), lambda b, r, i: (b, r, i, col))

    def prev(col):
        return pl.BlockSpec((None, None, BAND, ha), lambda b, r, i: (b, r, jnp.maximum(qb * i - 1, 0), col))

    n_heads = len(slopes)
    return pl.pallas_call(
        functools.partial(_dil_kernel, dil=dil, slopes=slopes, qb=qb),
        grid=(batch, dil, nqb // qb),
        in_specs=[own(0), prev(1), own(1), prev(2), own(2)],
        out_specs=[own(0), pl.BlockSpec((None, None, qb * BAND, LANES), lambda b, r, i: (b, r, i, 0))],
        out_shape=[jax.ShapeDtypeStruct((batch, dil, n_rows, ha), BF16),
                   jax.ShapeDtypeStruct((batch, dil, n_rows, LANES), F32)],
        scratch_shapes=[pltpu.VMEM((qb * n_heads, BAND, 2 * BAND), F32),
                        pltpu.VMEM((qb * n_heads, BAND, 2 * BAND), BF16)],
        compiler_params=_params("parallel", "parallel", "arbitrary"),
        name=f"dilated_d{dil}",
    )(h_d, h_d, h_d, h_d, h_d)


def _dil_combine_kernel(*refs, n_heads, dils):
    n = len(dils)
    o_refs, l_refs = refs[:n], refs[n:2 * n]
    g_ref, out_ref = refs[2 * n], refs[2 * n + 1]
    scratch = refs[2 * n + 2:]
    tm = out_ref.shape[0]
    outs, lses = [], []
    for p, d in enumerate(dils):
        o_sc, l_sc = scratch[2 * p], scratch[2 * p + 1]
        for r in range(d):
            rows = pl.ds(r, tm // d, stride=d)
            for h in range(n_heads):
                o_sc[h, rows, :] = o_refs[p][r, :, h * HEAD_DIM:(h + 1) * HEAD_DIM].astype(F32)
            l_sc[rows, :] = l_refs[p][r]
        outs.append(o_sc)
        lses.append(l_sc[...])
    mx = functools.reduce(jnp.maximum, lses)
    es = [jnp.exp(l - mx) for l in lses]
    inv = 1.0 / functools.reduce(jnp.add, es)
    ws = [e * inv for e in es]
    for h in range(n_heads):
        sl = slice(h * HEAD_DIM, (h + 1) * HEAD_DIM)
        o = functools.reduce(jnp.add, [w[:, h:h + 1] * o_sc[h] for w, o_sc in zip(ws, outs)])
        out_ref[:, sl] = _head_rmsnorm(o, g_ref[:, sl]).astype(out_ref.dtype)


def dilated_combine(outs, lses, gain, dils):
    batch, _, seq, ha = outs[0].shape
    seq *= dils[0]
    tm = 256
    assert seq % tm == 0 and tm % (SUBLANES * max(dils)) == 0
    nt = seq // tm

    def spec(d, width):
        return pl.BlockSpec((None, d, tm // d, width), lambda b, i: (b, 0, i, 0))

    scratch = []
    for _ in dils:
        scratch += [pltpu.VMEM((ha // HEAD_DIM, tm, HEAD_DIM), F32), pltpu.VMEM((tm, LANES), F32)]
    return pl.pallas_call(
        functools.partial(_dil_combine_kernel, n_heads=ha // HEAD_DIM, dils=dils),
        grid=(batch, nt),
        in_specs=([spec(d, ha) for d in dils] + [spec(d, LANES) for d in dils]
                  + [pl.BlockSpec((1, ha), lambda b, i: (0, 0))]),
        out_specs=pl.BlockSpec((tm, ha), lambda b, i: (b * nt + i, 0)),
        out_shape=jax.ShapeDtypeStruct((batch * seq, ha), BF16),
        scratch_shapes=scratch,
        compiler_params=_params("parallel", "parallel"),
        name="dilated_combine",
    )(*outs, *lses, gain.reshape(1, ha))


def _compress_kernel(x_ref, pe_ref, w1_ref, w2_ref, o_ref):
    x = (x_ref[...].astype(F32) + pe_ref[...]).astype(BF16)
    hid = jnp.dot(x, w1_ref[...], preferred_element_type=F32)
    act = 0.5 * hid * (1.0 + jnp.tanh(np.sqrt(2.0 / np.pi) * (hid + 0.044715 * (hid * hid * hid))))
    o_ref[...] = jnp.dot(act.astype(BF16), w2_ref[...], preferred_element_type=F32).astype(o_ref.dtype)


def compress(blocks, pe, w1, w2):
    _, rows, width = blocks.shape
    tr = _pick(rows, (256, 128, SUBLANES))
    return pl.pallas_call(
        _compress_kernel,
        grid=(2, rows // tr),
        in_specs=[pl.BlockSpec((None, tr, width), lambda s, i: (s, i, 0)),
                  pl.BlockSpec((None, 1, width), lambda s, i: (s, 0, 0)),
                  pl.BlockSpec((None, width, HEAD_DIM), lambda s, i: (s, 0, 0)),
                  pl.BlockSpec((None, HEAD_DIM, HEAD_DIM), lambda s, i: (s, 0, 0))],
        out_specs=pl.BlockSpec((None, tr, HEAD_DIM), lambda s, i: (s, i, 0)),
        out_shape=jax.ShapeDtypeStruct((2, rows, HEAD_DIM), BF16),
        compiler_params=_params("parallel", "parallel"),
        name="nsa_compress",
    )(blocks, pe, w1, w2)


def _nsa_kernel(q_ref, kc_ref, vc_ref, ks_ref, vs_ref, kw_ref, vw_ref, ov_ref, z_ref, g_ref, o_ref,
                q_sc, sel_sc, m_sc, l_sc, acc_sc, *, tq, tk, n_cmp, n_sel, slopes):
    g = pl.program_id(1)
    i = pl.program_id(2)
    rep = len(slopes[0])
    sel_shift = int(np.log2(SEL_BLOCK))

    for r in range(rep):
        q_sc[r * tq:(r + 1) * tq, :] = q_ref[:, r * HEAD_DIM:(r + 1) * HEAD_DIM]
    head_slopes = [jnp.where(g == 0, slopes[0][r] * LOG2E, slopes[1][r] * LOG2E) for r in range(rep)]

    def biased(s, dist, ok):
        dist_f = dist.astype(F32)
        return jnp.concatenate(
            [jnp.where(ok, s[r * tq:(r + 1) * tq] * (SCALE * LOG2E) - head_slopes[r] * dist_f, NEG_INF)
             for r in range(rep)], axis=0)

    def rel_pos(k0, width):
        return (lax.broadcasted_iota(jnp.int32, (tq, width), 0)
                - lax.broadcasted_iota(jnp.int32, (tq, width), 1)) + (i * tq - k0)

    t_pos = i * tq + lax.broadcasted_iota(jnp.int32, (tq, n_cmp), 0)
    c_dist = t_pos - (lax.broadcasted_iota(jnp.int32, (tq, n_cmp), 1) * CMP_STRIDE + (CMP_BLOCK - 1))
    s = lax.dot_general(q_sc[...], kc_ref[...], _NT, preferred_element_type=F32)
    s = biased(s, c_dist, c_dist >= 0)
    e = jnp.exp2(s - jnp.max(s, axis=-1, keepdims=True))
    p = e * (1.0 / jnp.sum(e, axis=-1, keepdims=True))
    o_cmp = jnp.dot(p.astype(BF16), vc_ref[...], preferred_element_type=F32)
    any_valid = jnp.where(i * tq + lax.broadcasted_iota(jnp.int32, (tq, 1), 0) >= CMP_BLOCK - 1, 1.0, 0.0)

    p_sum = p[0:tq]
    for r in range(1, rep):
        p_sum = p_sum + p[r * tq:(r + 1) * tq]
    p_sum = p_sum * any_valid
    imp = lax.dot_general(ov_ref[...], p_sum.astype(BF16), _NT, preferred_element_type=F32)
    blk = lax.broadcasted_iota(jnp.int32, (n_sel, tq), 0)
    cur = lax.shift_right_logical(i * tq + lax.broadcasted_iota(jnp.int32, (n_sel, tq), 1), sel_shift)
    imp = jnp.where(blk <= cur, imp, -1.0)
    imp = jnp.where(blk == 0, FORCE_SCORE, jnp.where(blk >= cur - 1, jnp.where(blk <= cur, FORCE_SCORE, imp), imp))
    rank = jnp.zeros((n_sel, tq), jnp.int32)
    for c in range(n_sel):
        row = imp[c:c + 1, :]
        tie = jnp.where(blk > c, 1, 0)
        rank = rank + jnp.where(row > imp, 1, jnp.where(row == imp, tie, 0))
    sel_t = jnp.where(rank < min(N_SELECT, n_sel), 1.0, 0.0)
    sel_sc[...] = jnp.transpose(sel_t).astype(BF16)

    _flash_reset(m_sc, l_sc, acc_sc)

    def slc_block(kb, carry):
        k0 = pl.multiple_of(kb * tk, tk)
        dist = rel_pos(k0, tk)
        kblk = lax.shift_right_logical(k0 + lax.broadcasted_iota(jnp.int32, (n_sel, tk), 1), sel_shift)
        expand = jnp.where(lax.broadcasted_iota(jnp.int32, (n_sel, tk), 0) == kblk, 1.0, 0.0).astype(BF16)
        picked = jnp.dot(sel_sc[...], expand, preferred_element_type=F32)
        s = lax.dot_general(q_sc[...], ks_ref[pl.ds(k0, tk), :], _NT, preferred_element_type=F32)
        s = biased(s, dist, (dist >= 0) & (picked > 0.5))
        _flash_step(s, vs_ref[pl.ds(k0, tk), :], m_sc, l_sc, acc_sc)
        return carry

    lax.fori_loop(0, (i * tq + tq - 1) // tk + 1, slc_block, 0)
    o_slc = acc_sc[...] * (1.0 / l_sc[...])

    wk = NSA_WINDOW + tq
    w0 = pl.multiple_of(jnp.maximum(i * tq - NSA_WINDOW, 0), tq)
    dist = rel_pos(w0, wk)
    s = lax.dot_general(q_sc[...], kw_ref[pl.ds(w0, wk), :], _NT, preferred_element_type=F32)
    s = biased(s, dist, (dist >= 0) & (dist <= NSA_WINDOW))
    e = jnp.exp2(s - jnp.max(s, axis=-1, keepdims=True))
    o_win = jnp.dot(e.astype(BF16), vw_ref[pl.ds(w0, wk), :], preferred_element_type=F32)
    o_win = o_win * (1.0 / jnp.sum(e, axis=-1, keepdims=True))

    gate = 1.0 / (1.0 + jnp.exp(-z_ref[...]))
    for r in range(rep):
        rs = slice(r * tq, (r + 1) * tq)
        sl = slice(r * HEAD_DIM, (r + 1) * HEAD_DIM)
        c = r * N_NSA_BRANCHES
        o = (o_cmp[rs] * (gate[:, c:c + 1] * any_valid) + o_slc[rs] * gate[:, c + 1:c + 2]
             + o_win[rs] * gate[:, c + 2:c + 3])
        o_ref[:, sl] = _head_rmsnorm(o, g_ref[:, sl]).astype(o_ref.dtype)


def native_sparse_attention(h_b, kv_cmp, h_small, gain, batch, seq, slopes):
    m = h_b.shape[0]
    rep = len(slopes[0])
    gw = rep * HEAD_DIM
    hb = NSA_KV_GROUPS * gw
    n_cmp = seq // CMP_STRIDE
    n_sel = seq // SEL_BLOCK
    tq = 256
    tk = _pick(seq, (512, 256, 128))
    nq = seq // tq
    assert seq >= NSA_WINDOW + tq
    col = hb // HEAD_DIM

    def seq_spec(branch):
        return pl.BlockSpec((seq, HEAD_DIM), lambda b, g, i: (b, col + branch * NSA_KV_GROUPS + g))

    def cmp_spec(which):
        return pl.BlockSpec((None, n_cmp, HEAD_DIM), lambda b, g, i: (which, b * NSA_KV_GROUPS + g, 0))

    rows = rep * tq
    return pl.pallas_call(
        functools.partial(_nsa_kernel, tq=tq, tk=tk, n_cmp=n_cmp, n_sel=n_sel, slopes=slopes),
        grid=(batch, NSA_KV_GROUPS, nq),
        in_specs=[pl.BlockSpec((tq, gw), lambda b, g, i: (b * nq + i, g)),
                  cmp_spec(0), cmp_spec(1), seq_spec(2), seq_spec(3), seq_spec(4), seq_spec(5),
                  pl.BlockSpec((n_sel, n_cmp), lambda b, g, i: (0, 0)),
                  pl.BlockSpec((tq, LANES), lambda b, g, i: (b * nq + i, g)),
                  pl.BlockSpec((1, gw), lambda b, g, i: (0, g))],
        out_specs=pl.BlockSpec((tq, gw), lambda b, g, i: (b * nq + i, g)),
        out_shape=jax.ShapeDtypeStruct((m, hb), BF16),
        scratch_shapes=[pltpu.VMEM((rows, HEAD_DIM), BF16), pltpu.VMEM((tq, n_sel), BF16),
                        pltpu.VMEM((rows, LANES), F32), pltpu.VMEM((rows, LANES), F32),
                        pltpu.VMEM((rows, HEAD_DIM), F32)],
        compiler_params=_params("parallel", "parallel", "arbitrary"),
        name="native_sparse_attention",
    )(h_b, kv_cmp, kv_cmp, h_b, h_b, h_b, h_b, _overlap_matrix(seq), h_small, gain.reshape(1, hb))


def _forget_cumsum_kernel(z_ref, b_ref, c_ref, carry_sc, *, tr):
    @pl.when(pl.program_id(1) == 0)
    def _():
        carry_sc[...] = jnp.zeros(carry_sc.shape, F32)

    z = z_ref[...] + b_ref[...]
    log_f = jnp.minimum(z, 0.0) - jnp.log1p(jnp.exp(-jnp.abs(z)))
    tri = jnp.where(lax.broadcasted_iota(jnp.int32, (tr, tr), 0)
                    >= lax.broadcasted_iota(jnp.int32, (tr, tr), 1), 1.0, 0.0)
    c = jnp.dot(tri, log_f, preferred_element_type=F32, precision=lax.Precision.HIGHEST) + carry_sc[0:1, :]
    c_ref[...] = c
    carry_sc[...] = jnp.broadcast_to(c[tr - 1:tr, :], carry_sc.shape)


def forget_cumsum(h_small, tile, bias_row, batch, seq):
    m = h_small.shape[0]
    tr = _pick(seq, (256, 128))
    nb = seq // tr
    return pl.pallas_call(
        functools.partial(_forget_cumsum_kernel, tr=tr),
        grid=(batch, nb),
        in_specs=[pl.BlockSpec((tr, LANES), lambda b, i: (b * nb + i, tile)),
                  pl.BlockSpec((1, LANES), lambda b, i: (0, 0))],
        out_specs=pl.BlockSpec((tr, LANES), lambda b, i: (b * nb + i, 0)),
        out_shape=jax.ShapeDtypeStruct((m, LANES), F32),
        scratch_shapes=[pltpu.VMEM((SUBLANES, LANES), F32)],
        compiler_params=_params("parallel", "arbitrary"),
        name="forget_cumsum",
    )(h_small, bias_row)


def _fox_kernel(q_ref, k_ref, v_ref, cq_ref, ck_ref, g_ref, o_ref, cq_sc, m_sc, l_sc, acc_sc,
                *, tq, tk, n_heads, hb):
    i = pl.program_id(1)
    n_full = (i * tq) // tk
    reps = tk // LANES
    for h0 in range(0, n_heads, hb):
        heads = list(range(h0, h0 + hb))
        for b, h in enumerate(heads):
            cq_sc[b] = jnp.broadcast_to(cq_ref[:, h:h + 1] * LOG2E, (tq, LANES))
            m_sc[b] = jnp.full((tq, LANES), NEG_INF, F32)
            l_sc[b] = jnp.zeros((tq, LANES), F32)
            acc_sc[b] = jnp.zeros((tq, HEAD_DIM), F32)

        def block(kb, masked):
            k0 = pl.multiple_of(kb * tk, tk)
            rows = pl.ds(k0, tk)
            qk = [lax.dot_general(q_ref[:, h * HEAD_DIM:(h + 1) * HEAD_DIM],
                                  k_ref[rows, h * HEAD_DIM:(h + 1) * HEAD_DIM], _NT,
                                  preferred_element_type=F32) for h in heads]
            if masked:
                causal = (k0 + lax.broadcasted_iota(jnp.int32, (tq, tk), 1)
                          <= i * tq + lax.broadcasted_iota(jnp.int32, (tq, tk), 0))
            for b, h in enumerate(heads):
                s = qk[b] * (SCALE * LOG2E) + _lane_tile(cq_sc[b], reps) - ck_ref[h:h + 1, rows] * LOG2E
                if masked:
                    s = jnp.where(causal, s, NEG_INF)
                m_prev = m_sc[b]
                m_new = jnp.maximum(m_prev, jnp.max(s, axis=-1, keepdims=True))
                alpha = jnp.exp2(m_prev - m_new)
                p = jnp.exp2(s - _lane_tile(m_new, reps)).astype(BF16)
                m_sc[b] = m_new
                pv = jnp.dot(p, _with_ones(v_ref[rows, h * HEAD_DIM:(h + 1) * HEAD_DIM]),
                             preferred_element_type=F32)
                acc_sc[b] = alpha * acc_sc[b] + pv[:, :HEAD_DIM]
                l_sc[b] = alpha * l_sc[b] + pv[:, HEAD_DIM:]

        def full_block(kb, carry):
            block(kb, False)
            return carry

        lax.fori_loop(0, n_full, full_block, 0)
        block(n_full, True)
        for b, h in enumerate(heads):
            sl = slice(h * HEAD_DIM, (h + 1) * HEAD_DIM)
            o = acc_sc[b] * (1.0 / l_sc[b])
            o_ref[:, sl] = _head_rmsnorm(o, g_ref[:, sl]).astype(o_ref.dtype)


def forgetting_attention(h_c, c_tok, c_head, gain, batch, seq):
    m, three_hc = h_c.shape
    hc = three_hc // 3
    hp = c_head.shape[0] // batch
    tq = _pick(seq, (512, 256, 128))
    tk = _pick(seq, (512, 256, 128))
    nq = seq // tq
    n_heads = hc // HEAD_DIM
    hb = _pick(n_heads, (4, 3, 2, 1))
    resident = functools.partial(pl.BlockSpec, pipeline_mode=pl.Buffered(1))
    return pl.pallas_call(
        functools.partial(_fox_kernel, tq=tq, tk=tk, n_heads=n_heads, hb=hb),
        grid=(batch, nq),
        in_specs=[pl.BlockSpec((tq, hc), lambda b, i: (b * nq + i, 0)),
                  resident((seq, hc), lambda b, i: (b, 1)),
                  resident((seq, hc), lambda b, i: (b, 2)),
                  pl.BlockSpec((tq, LANES), lambda b, i: (b * nq + i, 0)),
                  resident((hp, seq), lambda b, i: (b, 0)),
                  pl.BlockSpec((1, hc), lambda b, i: (0, 0))],
        out_specs=pl.BlockSpec((tq, hc), lambda b, i: (b * nq + i, 0)),
        out_shape=jax.ShapeDtypeStruct((m, hc), BF16),
        scratch_shapes=[pltpu.VMEM((hb, tq, LANES), F32), pltpu.VMEM((hb, tq, LANES), F32),
                        pltpu.VMEM((hb, tq, LANES), F32), pltpu.VMEM((hb, tq, HEAD_DIM), F32)],
        compiler_params=_params("parallel", "arbitrary"),
        name="forgetting_attention",
    )(h_c, h_c, h_c, c_tok, c_head, gain.reshape(1, hc))


def _alibi_slopes(h_dil, h_nsa):
    n = h_dil + h_nsa
    s = (2.0 ** (-8.0 * np.arange(1, n + 1) / n)).astype(np.float32)
    nsa_mask = np.isin(np.arange(n) % 5, [1, 3])
    dil = tuple(float(v) for v in s[~nsa_mask])
    nsa = s[nsa_mask].reshape(NSA_KV_GROUPS, h_nsa // NSA_KV_GROUPS)
    return dil, tuple(tuple(float(v) for v in row) for row in nsa)


def _overlap_matrix(seq):
    n_cmp = seq // CMP_STRIDE
    n_sel = seq // SEL_BLOCK
    c_start = np.arange(n_cmp) * CMP_STRIDE
    s_start = np.arange(n_sel) * SEL_BLOCK
    ov = ((c_start[None, :] < s_start[:, None] + SEL_BLOCK)
          & (c_start[None, :] + CMP_BLOCK > s_start[:, None]))
    return jnp.asarray(ov, BF16)


def _head_split(d):
    n_heads = d // HEAD_DIM
    h_dil = 3 * n_heads // 8
    h_nsa = n_heads // 4
    return h_dil, h_nsa, n_heads - h_dil - h_nsa


def _in_proj_columns(d):
    h_dil, h_nsa, h_fox = _head_split(d)
    b0 = 3 * h_dil * HEAD_DIM
    g0 = b0 + (h_nsa + 6 * NSA_KV_GROUPS) * HEAD_DIM
    c0 = g0 + h_nsa * N_NSA_BRANCHES
    f0 = c0 + 3 * h_fox * HEAD_DIM
    return b0, g0, c0, f0


def _small_projection(w_in, d):
    depth = w_in.shape[0]
    _, h_nsa, h_fox = _head_split(d)
    _, g0, _, f0 = _in_proj_columns(d)
    per_group = h_nsa // NSA_KV_GROUPS * N_NSA_BRANCHES
    tiles = []
    for g in range(NSA_KV_GROUPS):
        tiles += [w_in[:, :, g0 + g * per_group:g0 + (g + 1) * per_group],
                  jnp.zeros((depth, d, LANES - per_group), w_in.dtype)]
    tiles += [w_in[:, :, f0:f0 + h_fox], jnp.zeros((depth, d, LANES - h_fox), w_in.dtype)]
    return jnp.concatenate(tiles, axis=2)


def _layer(x, batch, seq, l, norm_mix, w_mix, w_small, b_forget, cmp_pe, cmp_w1, cmp_w2, head_norm,
           w_out, norm_ffn, w_gate, w_up, w_down):
    m, d = x.shape
    h_dil, h_nsa, h_fox = _head_split(d)
    ha, hb, hc, gkv = h_dil * HEAD_DIM, h_nsa * HEAD_DIM, h_fox * HEAD_DIM, NSA_KV_GROUPS * HEAD_DIM
    slopes_dil, slopes_nsa = _alibi_slopes(h_dil, h_nsa)
    b0, g0, _, _ = _in_proj_columns(d)
    bias_row = jnp.zeros((1, LANES), F32).at[0, :h_fox].set(b_forget.astype(F32))

    xn = rmsnorm(x, norm_mix, BF16)
    dils = tuple(d for _, d in DILATED_PATTERNS)
    h_a = matmul_dilated(xn, w_mix, l, b0, batch, seq, dils)
    h_b = matmul(xn, w_mix, l, b0, g0 - b0, BF16, name="in_proj_b")
    h_c = matmul(xn, w_mix, l, g0, 3 * hc, BF16, name="in_proj_c")
    h_small = matmul(xn, w_small, l, 0, w_small.shape[2], F32, name="in_proj_small")

    outs, lses = [], []
    for h_d, dil in zip(h_a, dils):
        o, lse = dilated_pattern(h_d, dil, slopes_dil)
        outs.append(o)
        lses.append(lse)
    o_a = dilated_combine(outs, lses, head_norm[:ha], dils)

    n_chunk = seq // CMP_STRIDE

    def cmp_blocks(col):
        a = h_b[:, col:col + gkv].reshape(batch, seq, NSA_KV_GROUPS, HEAD_DIM).transpose(0, 2, 1, 3)
        chunks = a.reshape(batch, NSA_KV_GROUPS, n_chunk, CMP_STRIDE * HEAD_DIM)
        nxt = jnp.concatenate([chunks[:, :, 1:], jnp.zeros_like(chunks[:, :, :1])], axis=2)
        return jnp.concatenate([chunks, nxt], axis=-1).reshape(batch * NSA_KV_GROUPS * n_chunk, -1)

    blocks = jnp.stack([cmp_blocks(hb), cmp_blocks(hb + gkv)])
    kv_cmp = compress(blocks, cmp_pe.reshape(2, 1, CMP_BLOCK * HEAD_DIM).astype(F32),
                      cmp_w1.astype(BF16), cmp_w2.astype(BF16))
    o_b = native_sparse_attention(h_b, kv_cmp, h_small, head_norm[ha:ha + hb], batch, seq, slopes_nsa)

    c_tok = forget_cumsum(h_small, NSA_KV_GROUPS, bias_row, batch, seq)
    hp = -(-h_fox // SUBLANES) * SUBLANES
    c_head = c_tok.reshape(batch, seq, LANES)[:, :, :h_fox].transpose(0, 2, 1)
    c_head = jnp.pad(c_head, ((0, 0), (0, hp - h_fox), (0, 0))).reshape(batch * hp, seq)
    o_c = forgetting_attention(h_c, c_tok, c_head, head_norm[ha + hb:], batch, seq)

    o = jnp.concatenate([o_a, o_b, o_c], axis=1)
    x = matmul(o, w_out, l, 0, d, F32, residual=x, name="out_proj")

    hf = rmsnorm(x, norm_ffn, BF16)
    gu = swiglu(hf, w_gate, w_up, l)
    k_blocks = 2 if gu.shape[1] % (2 * LANES) == 0 else 1
    for kb in range(k_blocks):
        x = matmul(gu, w_down, l, 0, d, F32, residual=x, k_block=kb, k_blocks=k_blocks, name="down_proj")
    return x


def kernel(x, norm_mix, w_in, b_forget, cmp_pe_k, cmp_w1_k, cmp_w2_k, cmp_pe_v, cmp_w1_v, cmp_w2_v,
           head_norm, w_out, norm_ffn, w_gate, w_up, w_down, norm_final):
    batch, seq, d = x.shape
    depth = norm_mix.shape[0]
    w_mix = in_proj_weights(w_in, d)
    w_small = _small_projection(w_in, d)
    h = x.reshape(batch * seq, d)
    for l in range(depth):
        h = _layer(h, batch, seq, l, norm_mix[l], w_mix, w_small, b_forget[l],
                   jnp.stack([cmp_pe_k[l], cmp_pe_v[l]]), jnp.stack([cmp_w1_k[l], cmp_w1_v[l]]),
                   jnp.stack([cmp_w2_k[l], cmp_w2_v[l]]), head_norm[l], w_out, norm_ffn[l],
                   w_gate, w_up, w_down)
    return rmsnorm(h, norm_final, x.dtype).reshape(batch, seq, d)
```

```python
import functools

import numpy as np
import jax
import jax.numpy as jnp
from jax import lax
from jax.experimental import pallas as pl
from jax.experimental.pallas import tpu as pltpu

F32 = jnp.float32
BF16 = jnp.bfloat16

HEAD_DIM = 128
LANES = 128
SUBLANES = 8
BF16_SUBLANES = 16
NSA_KV_GROUPS = 2
DILATED_PATTERNS = ((128, 1), (512, 4), (2048, 16))
BAND = 128
CMP_BLOCK = 32
CMP_STRIDE = 16
SEL_BLOCK = 64
N_SELECT = 16
NSA_WINDOW = 512
N_NSA_BRANCHES = 3
FORCE_SCORE = 1e9
NEG_INF = -1e30
NORM_EPS = 1e-6
SCALE = HEAD_DIM ** -0.5
LOG2E = float(np.log2(np.e))
VMEM_LIMIT = 58 * 1024 * 1024
MM_VMEM_BUDGET = 51 * 1024 * 1024

_NT = (((1,), (1,)), ((), ()))


def _params(*sem):
    return pltpu.CompilerParams(dimension_semantics=sem, vmem_limit_bytes=VMEM_LIMIT)


def _pick(n, prefs):
    for t in prefs:
        if n % t == 0:
            return t
    return n


def _rmsnorm_kernel(x_ref, g_ref, o_ref):
    x = x_ref[...].astype(F32)
    ms = jnp.mean(x * x, axis=-1, keepdims=True)
    o_ref[...] = (x * lax.rsqrt(ms + NORM_EPS) * g_ref[...]).astype(o_ref.dtype)


def rmsnorm(x, g, out_dtype):
    m, d = x.shape
    tm = _pick(m, (512, 256, 128, SUBLANES))
    return pl.pallas_call(
        _rmsnorm_kernel,
        grid=(m // tm,),
        in_specs=[pl.BlockSpec((tm, d), lambda i: (i, 0)),
                  pl.BlockSpec((1, d), lambda i: (0, 0))],
        out_specs=pl.BlockSpec((tm, d), lambda i: (i, 0)),
        out_shape=jax.ShapeDtypeStruct((m, d), out_dtype),
        compiler_params=_params("parallel"),
        name="rmsnorm",
    )(x, g.reshape(1, d).astype(F32))


def _staged(w_ref, wb_sc):
    if wb_sc is None:
        return w_ref

    @pl.when(pl.program_id(1) == 0)
    def _():
        wb_sc[...] = w_ref[...].astype(BF16)

    return wb_sc


def _mm_kernel(a_ref, w_ref, o_ref, wb_sc=None):
    w = _staged(w_ref, wb_sc)
    o_ref[...] = jnp.dot(a_ref[...], w[...], preferred_element_type=F32).astype(o_ref.dtype)


def _mm_res_kernel(a_ref, w_ref, r_ref, o_ref, wb_sc=None):
    w = _staged(w_ref, wb_sc)
    acc = jnp.dot(a_ref[...], w[...], preferred_element_type=F32)
    o_ref[...] = (r_ref[...] + acc).astype(o_ref.dtype)


def _mm_parts_res_kernel(*refs, n_parts):
    a_refs = refs[:n_parts]
    w_ref, r_ref, o_ref, wb_sc = refs[n_parts:]
    w = _staged(w_ref, wb_sc)
    acc = r_ref[...]
    row = 0
    for a_ref in a_refs:
        width = a_ref.shape[1]
        acc = acc + jnp.dot(a_ref[...], w[row:row + width, :], preferred_element_type=F32)
        row += width
    o_ref[...] = acc.astype(o_ref.dtype)


def matmul_parts(parts, w, layer, out_dtype, residual, name):
    m = parts[0].shape[0]
    k = sum(p.shape[1] for p in parts)
    n = w.shape[2]
    out_bytes = jnp.dtype(out_dtype).itemsize + residual.dtype.itemsize
    tm, tn, w_bufs = _mm_tiles(m, n, k, 0, 1, out_bytes, w.dtype.itemsize)
    return pl.pallas_call(
        functools.partial(_mm_parts_res_kernel, n_parts=len(parts)),
        grid=(n // tn, m // tm),
        in_specs=([pl.BlockSpec((tm, p.shape[1]), lambda j, i: (i, 0)) for p in parts]
                  + [_weight_spec(k, tn, w_bufs, layer, 0, 0),
                     pl.BlockSpec((tm, tn), lambda j, i: (i, j))]),
        out_specs=pl.BlockSpec((tm, tn), lambda j, i: (i, j)),
        out_shape=jax.ShapeDtypeStruct((m, n), out_dtype),
        scratch_shapes=_staging(w, k, tn),
        compiler_params=_params("parallel", "arbitrary"),
        name=name,
    )(*parts, w, residual)


def _swiglu_kernel(a_ref, wg_ref, wu_ref, o_ref, wg_sc, wu_sc):
    wg = _staged(wg_ref, wg_sc)
    wu = _staged(wu_ref, wu_sc)
    a = a_ref[...]
    g = jnp.dot(a, wg[...], preferred_element_type=F32)
    u = jnp.dot(a, wu[...], preferred_element_type=F32)
    o_ref[...] = (g * (1.0 / (1.0 + jnp.exp(-g))) * u).astype(o_ref.dtype)


def _mm_tiles(m, n, k, col0, n_weights, out_bytes, w_bytes):
    best = None
    staged = 2 if w_bytes == 4 else 0
    for tn in (1024, 512, 256, 128) + ((n,) if n < 512 and col0 == 0 else ()):
        if n % tn or col0 % tn:
            continue
        for tm in (1024, 512, 256, 128):
            if m % tm:
                continue
            rest = 2 * tm * k * 2 + 2 * tm * tn * out_bytes + (n_weights + 1) * tm * tn * 4
            for w_bufs in (2, 1):
                need = n_weights * k * tn * (w_bytes * w_bufs + staged) + rest
                if need <= MM_VMEM_BUDGET and (best is None or (tm * tn, w_bufs) > (best[0] * best[1], best[2])):
                    best = (tm, tn, w_bufs)
    assert best is not None, (m, n, k, col0)
    return best


def _weight_spec(k, tn, w_bufs, layer, k_block, col_block0):
    return pl.BlockSpec((None, k, tn), lambda j, i: (layer, k_block, col_block0 + j),
                        pipeline_mode=pl.Buffered(w_bufs))


def _staging(w, k, tn, n_weights=1):
    return [pltpu.VMEM((k, tn), BF16)] * n_weights if w.dtype == F32 else []


def matmul(a, w, layer, col0, n, out_dtype, residual=None, k_block=0, k_blocks=1, name="matmul"):
    m = a.shape[0]
    k = a.shape[1] // k_blocks
    out_bytes = jnp.dtype(out_dtype).itemsize + (0 if residual is None else residual.dtype.itemsize)
    tm, tn, w_bufs = _mm_tiles(m, n, k, col0, 1, out_bytes, w.dtype.itemsize)
    in_specs = [pl.BlockSpec((tm, k), lambda j, i: (i, k_block)),
                _weight_spec(k, tn, w_bufs, layer, k_block, col0 // tn)]
    args = [a, w]
    kern = _mm_kernel
    if residual is not None:
        in_specs.append(pl.BlockSpec((tm, tn), lambda j, i: (i, j)))
        args.append(residual)
        kern = _mm_res_kernel
    return pl.pallas_call(
        kern,
        grid=(n // tn, m // tm),
        in_specs=in_specs,
        out_specs=pl.BlockSpec((tm, tn), lambda j, i: (i, j)),
        out_shape=jax.ShapeDtypeStruct((m, n), out_dtype),
        scratch_shapes=_staging(w, k, tn),
        compiler_params=_params("parallel", "arbitrary"),
        name=name,
    )(*args)


def _mm_dilated_kernel(a_ref, w_ref, *refs, dils):
    out_refs, scratch = refs[:len(dils)], refs[len(dils):]
    res_sc = scratch[-1]
    w = _staged(w_ref, scratch[0] if len(scratch) > 1 else None)
    res = jnp.dot(a_ref[...], w[...], preferred_element_type=F32)
    n_chunks, tm, _ = res_sc.shape
    for c in range(n_chunks):
        res_sc[c] = res[:, c * LANES:(c + 1) * LANES]
    for o_ref, d in zip(out_refs, dils):
        if d == 1:
            o_ref[0] = res.astype(o_ref.dtype)
            continue
        for r in range(d):
            for c in range(n_chunks):
                o_ref[r, :, c * LANES:(c + 1) * LANES] = (
                    res_sc[c, pl.ds(r, tm // d, stride=d), :].astype(o_ref.dtype))


def matmul_dilated(a, w, layer, n, batch, seq, dils):
    m, k = a.shape
    tm, tn, w_bufs = _mm_tiles(m, n, k, 0, 1, 2 * len(dils) + 4, w.dtype.itemsize)
    assert seq % tm == 0 and tm % (BF16_SUBLANES * max(dils)) == 0
    per_batch = seq // tm
    return pl.pallas_call(
        functools.partial(_mm_dilated_kernel, dils=dils),
        grid=(n // tn, m // tm),
        in_specs=[pl.BlockSpec((tm, k), lambda j, i: (i, 0)),
                  _weight_spec(k, tn, w_bufs, layer, 0, 0)],
        out_specs=[pl.BlockSpec((None, d, tm // d, tn), lambda j, i: (i // per_batch, 0, i % per_batch, j))
                   for d in dils],
        out_shape=[jax.ShapeDtypeStruct((batch, d, seq // d, n), BF16) for d in dils],
        scratch_shapes=_staging(w, k, tn) + [pltpu.VMEM((tn // LANES, tm, LANES), F32)],
        compiler_params=_params("parallel", "arbitrary"),
        name="in_proj_a",
    )(a, w)


def _in_proj_prep_kernel(w_ref, o_ref):
    for layer in range(o_ref.shape[0]):
        o_ref[layer] = jnp.transpose(w_ref[:, layer, :]).astype(o_ref.dtype)


def in_proj_weights(w_in, d):
    depth, k, _ = w_in.shape
    _, g0, c0, f0 = _in_proj_columns(d)
    tn = 256 if g0 % 256 == 0 and (f0 - c0) % 256 == 0 else LANES
    n_ab = g0 // tn

    def source_rows(t):
        return (jnp.where(t < n_ab, t * tn, c0 + (t - n_ab) * tn), 0, 0)

    return pl.pallas_call(
        _in_proj_prep_kernel,
        grid=((g0 + f0 - c0) // tn,),
        in_specs=[pl.BlockSpec((pl.Element(tn), pl.Element(depth), pl.Element(k)), source_rows)],
        out_specs=pl.BlockSpec((depth, k, tn), lambda t: (0, 0, t)),
        out_shape=jax.ShapeDtypeStruct((depth, k, g0 + f0 - c0), BF16),
        compiler_params=_params("parallel"),
        name="in_proj_weights",
    )(jnp.transpose(w_in, (2, 0, 1)))


def swiglu(a, wg, wu, layer):
    m, k = a.shape
    n = wg.shape[2]
    tm, tn, w_bufs = _mm_tiles(m, n, k, 0, 2, 2, 4)
    w_spec = _weight_spec(k, tn, w_bufs, layer, 0, 0)
    return pl.pallas_call(
        _swiglu_kernel,
        grid=(n // tn, m // tm),
        in_specs=[pl.BlockSpec((tm, k), lambda j, i: (i, 0)), w_spec, w_spec],
        out_specs=pl.BlockSpec((tm, tn), lambda j, i: (i, j)),
        out_shape=jax.ShapeDtypeStruct((m, n), BF16),
        scratch_shapes=[pltpu.VMEM((k, tn), BF16), pltpu.VMEM((k, tn), BF16)],
        compiler_params=_params("parallel", "arbitrary"),
        name="swiglu",
    )(a, wg, wu)


def _head_rmsnorm(o, g):
    ms = jnp.mean(o * o, axis=-1, keepdims=True)
    return o * lax.rsqrt(ms + NORM_EPS) * g


def _lane_tile(x, reps):
    return x if reps == 1 else jnp.concatenate([x] * reps, axis=1)


def _with_ones(v):
    return jnp.concatenate([v, jnp.ones((v.shape[0], LANES), v.dtype)], axis=1)


def _flash_step(s, v, m_sc, l_sc, acc_sc):
    m_prev = m_sc[...]
    m_new = jnp.maximum(m_prev, jnp.max(s, axis=-1, keepdims=True))
    alpha = jnp.exp2(m_prev - m_new)
    p = jnp.exp2(s - _lane_tile(m_new, s.shape[1] // LANES))
    l_sc[...] = alpha * l_sc[...] + jnp.sum(p, axis=-1, keepdims=True)
    acc_sc[...] = alpha * acc_sc[...] + jnp.dot(p.astype(BF16), v, preferred_element_type=F32)
    m_sc[...] = m_new


def _flash_reset(m_sc, l_sc, acc_sc):
    m_sc[...] = jnp.full(m_sc.shape, NEG_INF, F32)
    l_sc[...] = jnp.zeros(l_sc.shape, F32)
    acc_sc[...] = jnp.zeros(acc_sc.shape, F32)


def _dil_kernel(q_ref, kp_ref, kc_ref, vp_ref, vc_ref, o_ref, lse_ref, s_sc, p_sc, *, dil, slopes, qb):
    i = pl.program_id(2)
    n_heads = len(slopes)
    qi = lax.broadcasted_iota(jnp.int32, (BAND, 2 * BAND), 0)
    kj = lax.broadcasted_iota(jnp.int32, (BAND, 2 * BAND), 1)
    dist = qi + BAND - kj
    in_band = (dist >= 0) & (dist <= BAND)
    pen = (dist * dil).astype(F32)
    lane = lax.broadcasted_iota(jnp.int32, (BAND, LANES), 1)

    def keys(prev_ref, own_ref, u, sl):
        if u == 0:
            return jnp.concatenate([prev_ref[:, sl], own_ref[0:BAND, sl]], axis=0)
        return own_ref[(u - 1) * BAND:(u + 1) * BAND, sl]

    for u in range(qb):
        for h in range(n_heads):
            sl = slice(h * HEAD_DIM, (h + 1) * HEAD_DIM)
            s_sc[u * n_heads + h] = lax.dot_general(q_ref[u * BAND:(u + 1) * BAND, sl], keys(kp_ref, kc_ref, u, sl),
                                                    _NT, preferred_element_type=F32)
    for u in range(qb):
        ok = in_band
        if u == 0:
            ok = ok & (kj >= jnp.where(i > 0, 0, BAND))
        lse_tile = jnp.zeros((BAND, LANES), F32)
        for h, slope in enumerate(slopes):
            s = jnp.where(ok, s_sc[u * n_heads + h] * SCALE - slope * pen, NEG_INF)
            m = jnp.max(s, axis=-1, keepdims=True)
            e = jnp.exp(s - m)
            den = jnp.sum(e, axis=-1, keepdims=True)
            p_sc[u * n_heads + h] = (e * (1.0 / den)).astype(BF16)
            lse_tile = jnp.where(lane == h, m + jnp.log(den), lse_tile)
        lse_ref[u * BAND:(u + 1) * BAND, :] = lse_tile
    for u in range(qb):
        for h in range(n_heads):
            sl = slice(h * HEAD_DIM, (h + 1) * HEAD_DIM)
            o = jnp.dot(p_sc[u * n_heads + h], keys(vp_ref, vc_ref, u, sl), preferred_element_type=F32)
            o_ref[u * BAND:(u + 1) * BAND, sl] = o.astype(o_ref.dtype)


def dilated_pattern(h_d, dil, slopes):
    batch, _, n_rows, three_ha = h_d.shape
    ha = three_ha // 3
    nqb = n_rows // BAND
    qb = 2 if nqb % 2 == 0 else 1

    def own(col):
        return pl.BlockSpec((None, None, qb * BAND, ha), lambda b, r, i: (b, r, i, col))

    def prev(col):
        return pl.BlockSpec((None, None, BAND, ha), lambda b, r, i: (b, r, jnp.maximum(qb * i - 1, 0), col))

    n_heads = len(slopes)
    return pl.pallas_call(
        functools.partial(_dil_kernel, dil=dil, slopes=slopes, qb=qb),
        grid=(batch, dil, nqb // qb),
        in_specs=[own(0), prev(1), own(1), prev(2), own(2)],
        out_specs=[own(0), pl.BlockSpec((None, None, qb * BAND, LANES), lambda b, r, i: (b, r, i, 0))],
        out_shape=[jax.ShapeDtypeStruct((batch, dil, n_rows, ha), BF16),
                   jax.ShapeDtypeStruct((batch, dil, n_rows, LANES), F32)],
        scratch_shapes=[pltpu.VMEM((qb * n_heads, BAND, 2 * BAND), F32),
                        pltpu.VMEM((qb * n_heads, BAND, 2 * BAND), BF16)],
        compiler_params=_params("parallel", "parallel", "arbitrary"),
        name=f"dilated_d{dil}",
    )(h_d, h_d, h_d, h_d, h_d)


def _dil_combine_kernel(*refs, n_heads, dils):
    n = len(dils)
    o_refs, l_refs = refs[:n], refs[n:2 * n]
    g_ref, out_ref = refs[2 * n], refs[2 * n + 1]
    scratch = refs[2 * n + 2:]
    tm = out_ref.shape[0]
    outs, lses = [], []
    for p, d in enumerate(dils):
        o_sc, l_sc = scratch[2 * p], scratch[2 * p + 1]
        for r in range(d):
            rows = pl.ds(r, tm // d, stride=d)
            for h in range(n_heads):
                o_sc[h, rows, :] = o_refs[p][r, :, h * HEAD_DIM:(h + 1) * HEAD_DIM].astype(F32)
            l_sc[rows, :] = l_refs[p][r]
        outs.append(o_sc)
        lses.append(l_sc[...])
    mx = functools.reduce(jnp.maximum, lses)
    es = [jnp.exp(l - mx) for l in lses]
    inv = 1.0 / functools.reduce(jnp.add, es)
    ws = [e * inv for e in es]
    for h in range(n_heads):
        sl = slice(h * HEAD_DIM, (h + 1) * HEAD_DIM)
        o = functools.reduce(jnp.add, [w[:, h:h + 1] * o_sc[h] for w, o_sc in zip(ws, outs)])
        out_ref[:, sl] = _head_rmsnorm(o, g_ref[:, sl]).astype(out_ref.dtype)


def dilated_combine(outs, lses, gain, dils):
    batch, _, seq, ha = outs[0].shape
    seq *= dils[0]
    tm = 256
    assert seq % tm == 0 and tm % (SUBLANES * max(dils)) == 0
    nt = seq // tm

    def spec(d, width):
        return pl.BlockSpec((None, d, tm // d, width), lambda b, i: (b, 0, i, 0))

    scratch = []
    for _ in dils:
        scratch += [pltpu.VMEM((ha // HEAD_DIM, tm, HEAD_DIM), F32), pltpu.VMEM((tm, LANES), F32)]
    return pl.pallas_call(
        functools.partial(_dil_combine_kernel, n_heads=ha // HEAD_DIM, dils=dils),
        grid=(batch, nt),
        in_specs=([spec(d, ha) for d in dils] + [spec(d, LANES) for d in dils]
                  + [pl.BlockSpec((1, ha), lambda b, i: (0, 0))]),
        out_specs=pl.BlockSpec((tm, ha), lambda b, i: (b * nt + i, 0)),
        out_shape=jax.ShapeDtypeStruct((batch * seq, ha), BF16),
        scratch_shapes=scratch,
        compiler_params=_params("parallel", "parallel"),
        name="dilated_combine",
    )(*outs, *lses, gain.reshape(1, ha))


def _compress_kernel(x_ref, pe_ref, w1_ref, w2_ref, o_ref):
    x = (x_ref[...].astype(F32) + pe_ref[...]).astype(BF16)
    hid = jnp.dot(x, w1_ref[...], preferred_element_type=F32)
    act = 0.5 * hid * (1.0 + jnp.tanh(np.sqrt(2.0 / np.pi) * (hid + 0.044715 * (hid * hid * hid))))
    o_ref[...] = jnp.dot(act.astype(BF16), w2_ref[...], preferred_element_type=F32).astype(o_ref.dtype)


def compress(blocks, pe, w1, w2):
    _, rows, width = blocks.shape
    tr = _pick(rows, (256, 128, SUBLANES))
    return pl.pallas_call(
        _compress_kernel,
        grid=(2, rows // tr),
        in_specs=[pl.BlockSpec((None, tr, width), lambda s, i: (s, i, 0)),
                  pl.BlockSpec((None, 1, width), lambda s, i: (s, 0, 0)),
                  pl.BlockSpec((None, width, HEAD_DIM), lambda s, i: (s, 0, 0)),
                  pl.BlockSpec((None, HEAD_DIM, HEAD_DIM), lambda s, i: (s, 0, 0))],
        out_specs=pl.BlockSpec((None, tr, HEAD_DIM), lambda s, i: (s, i, 0)),
        out_shape=jax.ShapeDtypeStruct((2, rows, HEAD_DIM), BF16),
        compiler_params=_params("parallel", "parallel"),
        name="nsa_compress",
    )(blocks, pe, w1, w2)


def _nsa_kernel(q_ref, kc_ref, vc_ref, ks_ref, vs_ref, kw_ref, vw_ref, ov_ref, z_ref, g_ref, o_ref,
                q_sc, sel_sc, m_sc, l_sc, acc_sc, *, tq, tk, n_cmp, n_sel, slopes):
    g = pl.program_id(1)
    i = pl.program_id(2)
    rep = len(slopes[0])
    sel_shift = int(np.log2(SEL_BLOCK))

    for r in range(rep):
        q_sc[r * tq:(r + 1) * tq, :] = q_ref[:, r * HEAD_DIM:(r + 1) * HEAD_DIM]
    head_slopes = [jnp.where(g == 0, slopes[0][r] * LOG2E, slopes[1][r] * LOG2E) for r in range(rep)]

    def biased(s, dist, ok):
        dist_f = dist.astype(F32)
        return jnp.concatenate(
            [jnp.where(ok, s[r * tq:(r + 1) * tq] * (SCALE * LOG2E) - head_slopes[r] * dist_f, NEG_INF)
             for r in range(rep)], axis=0)

    def rel_pos(k0, width):
        return (lax.broadcasted_iota(jnp.int32, (tq, width), 0)
                - lax.broadcasted_iota(jnp.int32, (tq, width), 1)) + (i * tq - k0)

    t_pos = i * tq + lax.broadcasted_iota(jnp.int32, (tq, n_cmp), 0)
    c_dist = t_pos - (lax.broadcasted_iota(jnp.int32, (tq, n_cmp), 1) * CMP_STRIDE + (CMP_BLOCK - 1))
    s = lax.dot_general(q_sc[...], kc_ref[...], _NT, preferred_element_type=F32)
    s = biased(s, c_dist, c_dist >= 0)
    e = jnp.exp2(s - jnp.max(s, axis=-1, keepdims=True))
    p = e * (1.0 / jnp.sum(e, axis=-1, keepdims=True))
    o_cmp = jnp.dot(p.astype(BF16), vc_ref[...], preferred_element_type=F32)
    any_valid = jnp.where(i * tq + lax.broadcasted_iota(jnp.int32, (tq, 1), 0) >= CMP_BLOCK - 1, 1.0, 0.0)

    p_sum = p[0:tq]
    for r in range(1, rep):
        p_sum = p_sum + p[r * tq:(r + 1) * tq]
    p_sum = p_sum * any_valid
    imp = lax.dot_general(ov_ref[...], p_sum.astype(BF16), _NT, preferred_element_type=F32)
    blk = lax.broadcasted_iota(jnp.int32, (n_sel, tq), 0)
    cur = lax.shift_right_logical(i * tq + lax.broadcasted_iota(jnp.int32, (n_sel, tq), 1), sel_shift)
    imp = jnp.where(blk <= cur, imp, -1.0)
    imp = jnp.where(blk == 0, FORCE_SCORE, jnp.where(blk >= cur - 1, jnp.where(blk <= cur, FORCE_SCORE, imp), imp))
    rank = jnp.zeros((n_sel, tq), jnp.int32)
    for c in range(n_sel):
        row = imp[c:c + 1, :]
        tie = jnp.where(blk > c, 1, 0)
        rank = rank + jnp.where(row > imp, 1, jnp.where(row == imp, tie, 0))
    sel_t = jnp.where(rank < min(N_SELECT, n_sel), 1.0, 0.0)
    sel_sc[...] = jnp.transpose(sel_t).astype(BF16)

    _flash_reset(m_sc, l_sc, acc_sc)

    def slc_block(kb, carry):
        k0 = pl.multiple_of(kb * tk, tk)
        dist = rel_pos(k0, tk)
        kblk = lax.shift_right_logical(k0 + lax.broadcasted_iota(jnp.int32, (n_sel, tk), 1), sel_shift)
        expand = jnp.where(lax.broadcasted_iota(jnp.int32, (n_sel, tk), 0) == kblk, 1.0, 0.0).astype(BF16)
        picked = jnp.dot(sel_sc[...], expand, preferred_element_type=F32)
        s = lax.dot_general(q_sc[...], ks_ref[pl.ds(k0, tk), :], _NT, preferred_element_type=F32)
        s = biased(s, dist, (dist >= 0) & (picked > 0.5))
        _flash_step(s, vs_ref[pl.ds(k0, tk), :], m_sc, l_sc, acc_sc)
        return carry

    lax.fori_loop(0, (i * tq + tq - 1) // tk + 1, slc_block, 0)
    o_slc = acc_sc[...] * (1.0 / l_sc[...])

    wk = NSA_WINDOW + tq
    w0 = pl.multiple_of(jnp.maximum(i * tq - NSA_WINDOW, 0), tq)
    dist = rel_pos(w0, wk)
    s = lax.dot_general(q_sc[...], kw_ref[pl.ds(w0, wk), :], _NT, preferred_element_type=F32)
    s = biased(s, dist, (dist >= 0) & (dist <= NSA_WINDOW))
    e = jnp.exp2(s - jnp.max(s, axis=-1, keepdims=True))
    o_win = jnp.dot(e.astype(BF16), vw_ref[pl.ds(w0, wk), :], preferred_element_type=F32)
    o_win = o_win * (1.0 / jnp.sum(e, axis=-1, keepdims=True))

    gate = 1.0 / (1.0 + jnp.exp(-z_ref[...]))
    for r in range(rep):
        rs = slice(r * tq, (r + 1) * tq)
        sl = slice(r * HEAD_DIM, (r + 1) * HEAD_DIM)
        c = r * N_NSA_BRANCHES
        o = (o_cmp[rs] * (gate[:, c:c + 1] * any_valid) + o_slc[rs] * gate[:, c + 1:c + 2]
             + o_win[rs] * gate[:, c + 2:c + 3])
        o_ref[:, sl] = _head_rmsnorm(o, g_ref[:, sl]).astype(o_ref.dtype)


def native_sparse_attention(h_b, kv_cmp, h_small, gain, batch, seq, slopes):
    m = h_b.shape[0]
    rep = len(slopes[0])
    gw = rep * HEAD_DIM
    hb = NSA_KV_GROUPS * gw
    n_cmp = seq // CMP_STRIDE
    n_sel = seq // SEL_BLOCK
    tq = 256
    tk = _pick(seq, (512, 256, 128))
    nq = seq // tq
    assert seq >= NSA_WINDOW + tq
    col = hb // HEAD_DIM

    def seq_spec(branch):
        return pl.BlockSpec((seq, HEAD_DIM), lambda b, g, i: (b, col + branch * NSA_KV_GROUPS + g))

    def cmp_spec(which):
        return pl.BlockSpec((None, n_cmp, HEAD_DIM), lambda b, g, i: (which, b * NSA_KV_GROUPS + g, 0))

    rows = rep * tq
    return pl.pallas_call(
        functools.partial(_nsa_kernel, tq=tq, tk=tk, n_cmp=n_cmp, n_sel=n_sel, slopes=slopes),
        grid=(batch, NSA_KV_GROUPS, nq),
        in_specs=[pl.BlockSpec((tq, gw), lambda b, g, i: (b * nq + i, g)),
                  cmp_spec(0), cmp_spec(1), seq_spec(2), seq_spec(3), seq_spec(4), seq_spec(5),
                  pl.BlockSpec((n_sel, n_cmp), lambda b, g, i: (0, 0)),
                  pl.BlockSpec((tq, LANES), lambda b, g, i: (b * nq + i, g)),
                  pl.BlockSpec((1, gw), lambda b, g, i: (0, g))],
        out_specs=pl.BlockSpec((tq, gw), lambda b, g, i: (b * nq + i, g)),
        out_shape=jax.ShapeDtypeStruct((m, hb), BF16),
        scratch_shapes=[pltpu.VMEM((rows, HEAD_DIM), BF16), pltpu.VMEM((tq, n_sel), BF16),
                        pltpu.VMEM((rows, LANES), F32), pltpu.VMEM((rows, LANES), F32),
                        pltpu.VMEM((rows, HEAD_DIM), F32)],
        compiler_params=_params("parallel", "parallel", "arbitrary"),
        name="native_sparse_attention",
    )(h_b, kv_cmp, kv_cmp, h_b, h_b, h_b, h_b, _overlap_matrix(seq), h_small, gain.reshape(1, hb))


def _forget_cumsum_kernel(z_ref, b_ref, c_ref, carry_sc, *, tr):
    @pl.when(pl.program_id(1) == 0)
    def _():
        carry_sc[...] = jnp.zeros(carry_sc.shape, F32)

    z = z_ref[...] + b_ref[...]
    log_f = jnp.minimum(z, 0.0) - jnp.log1p(jnp.exp(-jnp.abs(z)))
    tri = jnp.where(lax.broadcasted_iota(jnp.int32, (tr, tr), 0)
                    >= lax.broadcasted_iota(jnp.int32, (tr, tr), 1), 1.0, 0.0)
    c = jnp.dot(tri, log_f, preferred_element_type=F32, precision=lax.Precision.HIGHEST) + carry_sc[0:1, :]
    c_ref[...] = c
    carry_sc[...] = jnp.broadcast_to(c[tr - 1:tr, :], carry_sc.shape)


def forget_cumsum(h_small, tile, bias_row, batch, seq):
    m = h_small.shape[0]
    tr = _pick(seq, (256, 128))
    nb = seq // tr
    return pl.pallas_call(
        functools.partial(_forget_cumsum_kernel, tr=tr),
        grid=(batch, nb),
        in_specs=[pl.BlockSpec((tr, LANES), lambda b, i: (b * nb + i, tile)),
                  pl.BlockSpec((1, LANES), lambda b, i: (0, 0))],
        out_specs=pl.BlockSpec((tr, LANES), lambda b, i: (b * nb + i, 0)),
        out_shape=jax.ShapeDtypeStruct((m, LANES), F32),
        scratch_shapes=[pltpu.VMEM((SUBLANES, LANES), F32)],
        compiler_params=_params("parallel", "arbitrary"),
        name="forget_cumsum",
    )(h_small, bias_row)


def _fox_kernel(q_ref, k_ref, v_ref, cq_ref, ck_ref, g_ref, o_ref, cq_sc, m_sc, l_sc, acc_sc,
                *, tq, tk, n_heads, hb):
    i = pl.program_id(1)
    n_full = (i * tq) // tk
    reps = tk // LANES
    for h0 in range(0, n_heads, hb):
        heads = list(range(h0, h0 + hb))
        for b, h in enumerate(heads):
            cq_sc[b] = jnp.broadcast_to(cq_ref[:, h:h + 1] * LOG2E, (tq, LANES))
            m_sc[b] = jnp.full((tq, LANES), NEG_INF, F32)
            l_sc[b] = jnp.zeros((tq, LANES), F32)
            acc_sc[b] = jnp.zeros((tq, HEAD_DIM), F32)

        def block(kb, masked):
            k0 = pl.multiple_of(kb * tk, tk)
            rows = pl.ds(k0, tk)
            qk = [lax.dot_general(q_ref[:, h * HEAD_DIM:(h + 1) * HEAD_DIM],
                                  k_ref[rows, h * HEAD_DIM:(h + 1) * HEAD_DIM], _NT,
                                  preferred_element_type=F32) for h in heads]
            if masked:
                causal = (k0 + lax.broadcasted_iota(jnp.int32, (tq, tk), 1)
                          <= i * tq + lax.broadcasted_iota(jnp.int32, (tq, tk), 0))
            for b, h in enumerate(heads):
                s = qk[b] * (SCALE * LOG2E) + _lane_tile(cq_sc[b], reps) - ck_ref[h:h + 1, rows] * LOG2E
                if masked:
                    s = jnp.where(causal, s, NEG_INF)
                m_prev = m_sc[b]
                m_new = jnp.maximum(m_prev, jnp.max(s, axis=-1, keepdims=True))
                alpha = jnp.exp2(m_prev - m_new)
                p = jnp.exp2(s - _lane_tile(m_new, reps)).astype(BF16)
                m_sc[b] = m_new
                pv = jnp.dot(p, _with_ones(v_ref[rows, h * HEAD_DIM:(h + 1) * HEAD_DIM]),
                             preferred_element_type=F32)
                acc_sc[b] = alpha * acc_sc[b] + pv[:, :HEAD_DIM]
                l_sc[b] = alpha * l_sc[b] + pv[:, HEAD_DIM:]

        def full_block(kb, carry):
            block(kb, False)
            return carry

        lax.fori_loop(0, n_full, full_block, 0)
        block(n_full, True)
        for b, h in enumerate(heads):
            sl = slice(h * HEAD_DIM, (h + 1) * HEAD_DIM)
            o = acc_sc[b] * (1.0 / l_sc[b])
            o_ref[:, sl] = _head_rmsnorm(o, g_ref[:, sl]).astype(o_ref.dtype)


def forgetting_attention(h_c, c_tok, c_head, gain, batch, seq):
    m, three_hc = h_c.shape
    hc = three_hc // 3
    hp = c_head.shape[0] // batch
    tq = _pick(seq, (512, 256, 128))
    tk = _pick(seq, (512, 256, 128))
    nq = seq // tq
    n_heads = hc // HEAD_DIM
    hb = _pick(n_heads, (4, 3, 2, 1))
    resident = functools.partial(pl.BlockSpec, pipeline_mode=pl.Buffered(1))
    return pl.pallas_call(
        functools.partial(_fox_kernel, tq=tq, tk=tk, n_heads=n_heads, hb=hb),
        grid=(batch, nq),
        in_specs=[pl.BlockSpec((tq, hc), lambda b, i: (b * nq + i, 0)),
                  resident((seq, hc), lambda b, i: (b, 1)),
                  resident((seq, hc), lambda b, i: (b, 2)),
                  pl.BlockSpec((tq, LANES), lambda b, i: (b * nq + i, 0)),
                  resident((hp, seq), lambda b, i: (b, 0)),
                  pl.BlockSpec((1, hc), lambda b, i: (0, 0))],
        out_specs=pl.BlockSpec((tq, hc), lambda b, i: (b * nq + i, 0)),
        out_shape=jax.ShapeDtypeStruct((m, hc), BF16),
        scratch_shapes=[pltpu.VMEM((hb, tq, LANES), F32), pltpu.VMEM((hb, tq, LANES), F32),
                        pltpu.VMEM((hb, tq, LANES), F32), pltpu.VMEM((hb, tq, HEAD_DIM), F32)],
        compiler_params=_params("parallel", "arbitrary"),
        name="forgetting_attention",
    )(h_c, h_c, h_c, c_tok, c_head, gain.reshape(1, hc))


def _alibi_slopes(h_dil, h_nsa):
    n = h_dil + h_nsa
    s = (2.0 ** (-8.0 * np.arange(1, n + 1) / n)).astype(np.float32)
    nsa_mask = np.isin(np.arange(n) % 5, [1, 3])
    dil = tuple(float(v) for v in s[~nsa_mask])
    nsa = s[nsa_mask].reshape(NSA_KV_GROUPS, h_nsa // NSA_KV_GROUPS)
    return dil, tuple(tuple(float(v) for v in row) for row in nsa)


def _overlap_matrix(seq):
    n_cmp = seq // CMP_STRIDE
    n_sel = seq // SEL_BLOCK
    c_start = np.arange(n_cmp) * CMP_STRIDE
    s_start = np.arange(n_sel) * SEL_BLOCK
    ov = ((c_start[None, :] < s_start[:, None] + SEL_BLOCK)
          & (c_start[None, :] + CMP_BLOCK > s_start[:, None]))
    return jnp.asarray(ov, BF16)


def _head_split(d):
    n_heads = d // HEAD_DIM
    h_dil = 3 * n_heads // 8
    h_nsa = n_heads // 4
    return h_dil, h_nsa, n_heads - h_dil - h_nsa


def _in_proj_columns(d):
    h_dil, h_nsa, h_fox = _head_split(d)
    b0 = 3 * h_dil * HEAD_DIM
    g0 = b0 + (h_nsa + 6 * NSA_KV_GROUPS) * HEAD_DIM
    c0 = g0 + h_nsa * N_NSA_BRANCHES
    f0 = c0 + 3 * h_fox * HEAD_DIM
    return b0, g0, c0, f0


def _small_projection(w_in, d):
    depth = w_in.shape[0]
    _, h_nsa, h_fox = _head_split(d)
    _, g0, _, f0 = _in_proj_columns(d)
    per_group = h_nsa // NSA_KV_GROUPS * N_NSA_BRANCHES
    tiles = []
    for g in range(NSA_KV_GROUPS):
        tiles += [w_in[:, :, g0 + g * per_group:g0 + (g + 1) * per_group],
                  jnp.zeros((depth, d, LANES - per_group), w_in.dtype)]
    tiles += [w_in[:, :, f0:f0 + h_fox], jnp.zeros((depth, d, LANES - h_fox), w_in.dtype)]
    return jnp.concatenate(tiles, axis=2)


def _layer(x, batch, seq, l, norm_mix, w_mix, w_small, b_forget, cmp_pe, cmp_w1, cmp_w2, head_norm,
           w_out, norm_ffn, w_gate, w_up, w_down):
    m, d = x.shape
    h_dil, h_nsa, h_fox = _head_split(d)
    ha, hb, hc, gkv = h_dil * HEAD_DIM, h_nsa * HEAD_DIM, h_fox * HEAD_DIM, NSA_KV_GROUPS * HEAD_DIM
    slopes_dil, slopes_nsa = _alibi_slopes(h_dil, h_nsa)
    b0, g0, _, _ = _in_proj_columns(d)
    bias_row = jnp.zeros((1, LANES), F32).at[0, :h_fox].set(b_forget.astype(F32))

    xn = rmsnorm(x, norm_mix, BF16)
    dils = tuple(d for _, d in DILATED_PATTERNS)
    h_a = matmul_dilated(xn, w_mix, l, b0, batch, seq, dils)
    h_b = matmul(xn, w_mix, l, b0, g0 - b0, BF16, name="in_proj_b")
    h_c = matmul(xn, w_mix, l, g0, 3 * hc, BF16, name="in_proj_c")
    h_small = matmul(xn, w_small, l, 0, w_small.shape[2], F32, name="in_proj_small")

    outs, lses = [], []
    for h_d, dil in zip(h_a, dils):
        o, lse = dilated_pattern(h_d, dil, slopes_dil)
        outs.append(o)
        lses.append(lse)
    o_a = dilated_combine(outs, lses, head_norm[:ha], dils)

    n_chunk = seq // CMP_STRIDE

    def cmp_blocks(col):
        a = h_b[:, col:col + gkv].reshape(batch, seq, NSA_KV_GROUPS, HEAD_DIM).transpose(0, 2, 1, 3)
        chunks = a.reshape(batch, NSA_KV_GROUPS, n_chunk, CMP_STRIDE * HEAD_DIM)
        nxt = jnp.concatenate([chunks[:, :, 1:], jnp.zeros_like(chunks[:, :, :1])], axis=2)
        return jnp.concatenate([chunks, nxt], axis=-1).reshape(batch * NSA_KV_GROUPS * n_chunk, -1)

    blocks = jnp.stack([cmp_blocks(hb), cmp_blocks(hb + gkv)])
    kv_cmp = compress(blocks, cmp_pe.reshape(2, 1, CMP_BLOCK * HEAD_DIM).astype(F32),
                      cmp_w1.astype(BF16), cmp_w2.astype(BF16))
    o_b = native_sparse_attention(h_b, kv_cmp, h_small, head_norm[ha:ha + hb], batch, seq, slopes_nsa)

    c_tok = forget_cumsum(h_small, NSA_KV_GROUPS, bias_row, batch, seq)
    hp = -(-h_fox // SUBLANES) * SUBLANES
    c_head = c_tok.reshape(batch, seq, LANES)[:, :, :h_fox].transpose(0, 2, 1)
    c_head = jnp.pad(c_head, ((0, 0), (0, hp - h_fox), (0, 0))).reshape(batch * hp, seq)
    o_c = forgetting_attention(h_c, c_tok, c_head, head_norm[ha + hb:], batch, seq)

    x = matmul_parts([o_a, o_b, o_c], w_out, l, F32, x, "out_proj")

    hf = rmsnorm(x, norm_ffn, BF16)
    gu = swiglu(hf, w_gate, w_up, l)
    k_blocks = 2 if gu.shape[1] % (2 * LANES) == 0 else 1
    for kb in range(k_blocks):
        x = matmul(gu, w_down, l, 0, d, F32, residual=x, k_block=kb, k_blocks=k_blocks, name="down_proj")
    return x


def kernel(x, norm_mix, w_in, b_forget, cmp_pe_k, cmp_w1_k, cmp_w2_k, cmp_pe_v, cmp_w1_v, cmp_w2_v,
           head_norm, w_out, norm_ffn, w_gate, w_up, w_down, norm_final):
    batch, seq, d = x.shape
    depth = norm_mix.shape[0]
    w_mix = in_proj_weights(w_in, d)
    w_small = _small_projection(w_in, d)
    h = x.reshape(batch * seq, d)
    for l in range(depth):
        h = _layer(h, batch, seq, l, norm_mix[l], w_mix, w_small, b_forget[l],
                   jnp.stack([cmp_pe_k[l], cmp_pe_v[l]]), jnp.stack([cmp_w1_k[l], cmp_w1_v[l]]),
                   jnp.stack([cmp_w2_k[l], cmp_w2_v[l]]), head_norm[l], w_out, norm_ffn[l],
                   w_gate, w_up, w_down)
    return rmsnorm(h, norm_final, x.dtype).reshape(batch, seq, d)
```
